```python
import jax, jax.numpy as jnp
from jax import lax
import numpy as np

D_MODEL = 4096
BATCH = 2
SEQ = 4096
DEPTH = 2

CTX_LEN = 256
GRID_W = 64
HEAD_DIM = 128
EPS = 1e-6
ROPE_THETA = 10000.0
Q_BLOCK = 128
N_MOD = 9
D_FF = 10240
NA_HEADS = 16
NA_WIN_H = 8
NA_WIN_W = 16
GQA_Q_HEADS = 16
GQA_KV_HEADS = 4
GQA_GROUP = GQA_Q_HEADS // GQA_KV_HEADS
NA_WIDTH = NA_HEADS * HEAD_DIM
GQA_Q_WIDTH = GQA_Q_HEADS * HEAD_DIM
GQA_KV_WIDTH = GQA_KV_HEADS * HEAD_DIM
EV_Q_COLS = NA_WIDTH + GQA_Q_WIDTH
EV_KV_COLS = 2 * NA_WIDTH + 2 * GQA_KV_WIDTH
MLA_HEADS = 32
MLA_Q_RANK = 1024
MLA_KV_RANK = 512
MLA_NOPE = 128
MLA_ROPE = 64
MLA_V = 128

kernel_name = "hybrid_natten_gqa_mla_macaron_dit"


def rms_norm(x, g):
    xf = x.astype(jnp.float32)
    y = xf * lax.rsqrt(jnp.mean(xf * xf, axis=-1, keepdims=True) + EPS)
    return (y * g.astype(jnp.float32)).astype(x.dtype)


def modulate(h, g, shift, scale):
    return rms_norm(h, g) * (1.0 + scale) + shift


def swiglu(h, w1, w2):
    gate, up = jnp.split(h @ w1, 2, axis=-1)
    return (jax.nn.silu(gate) * up) @ w2


def heads(t, n):
    return t.reshape(*t.shape[:-1], n, -1)


def axial_angles(n_tok, rot_dim):
    t = jnp.arange(n_tok)
    row = (t // GRID_W).astype(jnp.float32)
    col = (t % GRID_W).astype(jnp.float32)
    axis_dim = rot_dim // 2
    inv_freq = ROPE_THETA ** (-jnp.arange(0, axis_dim, 2, dtype=jnp.float32) / axis_dim)
    ang = jnp.concatenate([row[:, None] * inv_freq, col[:, None] * inv_freq], axis=-1)
    return jnp.cos(ang), jnp.sin(ang)


def apply_rope(x, cos, sin):
    x1, x2 = jnp.split(x, 2, axis=-1)
    cs = cos[None, :, None, :].astype(x.dtype)
    sn = sin[None, :, None, :].astype(x.dtype)
    return jnp.concatenate([x1 * cs - x2 * sn, x1 * sn + x2 * cs], axis=-1)


def sweep_attention(q, k, v):
    B, S = q.shape[0], q.shape[1]
    scale = q.shape[-1] ** -0.5
    nb = S // Q_BLOCK
    qb = q.reshape(B, nb, Q_BLOCK, *q.shape[2:]).swapaxes(0, 1)

    def one_block(qblk):
        s = jnp.einsum("bqhgd,bkhd->bhgqk", qblk, k).astype(jnp.float32) * scale
        p = jax.nn.softmax(s, axis=-1).astype(v.dtype)
        return jnp.einsum("bhgqk,bkhd->bqhgd", p, v)

    ob = lax.map(one_block, qb)
    return ob.swapaxes(0, 1).reshape(B, S, *ob.shape[3:])


def neighbourhood_attention(q, k, v, k_ctx, v_ctx, rpb):
    B, S, H, Dh = q.shape
    rows = S // GRID_W
    kh = min(NA_WIN_H, rows)
    kw = NA_WIN_W
    scale = Dh ** -0.5
    qg = q.reshape(B, rows, GRID_W, H, Dh)
    kg = k.reshape(B, rows, GRID_W, H, Dh)
    vg = v.reshape(B, rows, GRID_W, H, Dh)
    col = jnp.arange(GRID_W)
    c0 = jnp.clip(col - kw // 2, 0, GRID_W - kw)
    col_mask = (col[None, :] >= c0[:, None]) & (col[None, :] < c0[:, None] + kw)
    col_idx = jnp.clip(col[None, :] - col[:, None] + NA_WIN_W - 1, 0, 2 * NA_WIN_W - 2)
    rpb_cols = rpb[:, :, col_idx]
    band = jnp.arange(kh)

    def one_row(r):
        r0 = jnp.clip(r - kh // 2, 0, rows - kh)
        qr = lax.dynamic_index_in_dim(qg, r, axis=1, keepdims=False)
        kb = lax.dynamic_slice_in_dim(kg, r0, kh, axis=1)
        vb = lax.dynamic_slice_in_dim(vg, r0, kh, axis=1)
        bias = jnp.take(rpb_cols, r0 + band - r + NA_WIN_H - 1, axis=1)
        s_band = jnp.einsum("bqhd,bikhd->bhqik", qr, kb).astype(jnp.float32) * scale
        s_band = s_band + bias.transpose(0, 2, 1, 3)[None].astype(jnp.float32)
        s_band = jnp.where(col_mask[None, None, :, None, :], s_band, -jnp.inf)
        s_ctx = jnp.einsum("bqhd,bchd->bhqc", qr, k_ctx).astype(jnp.float32) * scale
        s = jnp.concatenate([s_band.reshape(B, H, GRID_W, kh * GRID_W), s_ctx], axis=-1)
        p = jax.nn.softmax(s, axis=-1).astype(v.dtype)
        p_band = p[..., :kh * GRID_W].reshape(B, H, GRID_W, kh, GRID_W)
        p_ctx = p[..., kh * GRID_W:]
        return (jnp.einsum("bhqik,bikhd->bqhd", p_band, vb)
                + jnp.einsum("bhqc,bchd->bqhd", p_ctx, v_ctx))

    out = lax.map(one_row, jnp.arange(rows))
    return out.transpose(1, 0, 2, 3, 4).reshape(B, S, H, Dh)


def even_mixer(ul, uc, w_in, w_out, na_q_g, na_k_g, rpb, gq_q_g, gq_k_g, cos, sin, ctx_out):
    B, S, _ = ul.shape

    def split_q(p):
        na_q, gq_q = jnp.split(p, [NA_WIDTH], axis=-1)
        return rms_norm(heads(na_q, NA_HEADS), na_q_g), rms_norm(heads(gq_q, GQA_Q_HEADS), gq_q_g)

    def split_kv(p):
        na_k, na_v, gq_k, gq_v = jnp.split(p, [NA_WIDTH, 2 * NA_WIDTH, 2 * NA_WIDTH + GQA_KV_WIDTH], axis=-1)
        return (rms_norm(heads(na_k, NA_HEADS), na_k_g), heads(na_v, NA_HEADS),
                rms_norm(heads(gq_k, GQA_KV_HEADS), gq_k_g), heads(gq_v, GQA_KV_HEADS))

    pl = ul @ w_in
    na_q, gq_q = split_q(pl[..., :EV_Q_COLS])
    na_k, na_v, gq_k, gq_v = split_kv(pl[..., EV_Q_COLS:])
    pc = uc @ (w_in if ctx_out else w_in[:, EV_Q_COLS:])
    na_kc, na_vc, gq_kc, gq_vc = split_kv(pc[..., -EV_KV_COLS:])
    a = neighbourhood_attention(na_q, na_k, na_v, na_kc, na_vc, rpb)
    q_rot = apply_rope(gq_q, cos, sin).reshape(B, S, GQA_KV_HEADS, GQA_GROUP, HEAD_DIM)
    keys = jnp.concatenate([gq_kc, apply_rope(gq_k, cos, sin)], axis=1)
    vals = jnp.concatenate([gq_vc, gq_v], axis=1)
    b = sweep_attention(q_rot, keys, vals)
    out_l = jnp.concatenate([a.reshape(B, S, NA_WIDTH), b.reshape(B, S, GQA_Q_WIDTH)], axis=-1) @ w_out
    if not ctx_out:
        return out_l, None
    L = uc.shape[1]
    na_qc, gq_qc = split_q(pc[..., :EV_Q_COLS])
    ac = sweep_attention(na_qc[:, :, :, None, :], na_kc, na_vc)
    bc = sweep_attention(gq_qc.reshape(B, L, GQA_KV_HEADS, GQA_GROUP, HEAD_DIM), gq_kc, gq_vc)
    out_c = jnp.concatenate([ac.reshape(B, L, NA_WIDTH), bc.reshape(B, L, GQA_Q_WIDTH)], axis=-1) @ w_out
    return out_l, out_c


def mla_mixer(ul, uc, w_down, q_a_g, kv_a_g, w_uq, w_ukv, qn_g, qr_g, kn_g, kr_g, w_o, cos, sin, ctx_out):
    B, S, _ = ul.shape

    def queries(cq):
        q = heads(rms_norm(cq, q_a_g) @ w_uq, MLA_HEADS)
        q_nope, q_rope = jnp.split(q, [MLA_NOPE], axis=-1)
        return rms_norm(q_nope, qn_g), rms_norm(q_rope, qr_g)

    def keys_values(ckv, k_rope):
        kv = heads(rms_norm(ckv, kv_a_g) @ w_ukv, MLA_HEADS)
        k_nope, v = jnp.split(kv, [MLA_NOPE], axis=-1)
        return rms_norm(k_nope, kn_g), rms_norm(k_rope, kr_g)[:, :, None, :], v

    def join(nope, rope):
        return jnp.concatenate([nope, jnp.broadcast_to(rope, nope.shape[:-1] + rope.shape[-1:])], axis=-1)

    dl = ul @ w_down
    q_nope, q_rope = queries(dl[..., :MLA_Q_RANK])
    k_nope, k_rope, v = keys_values(dl[..., MLA_Q_RANK:MLA_Q_RANK + MLA_KV_RANK], dl[..., MLA_Q_RANK + MLA_KV_RANK:])
    dc = uc @ (w_down if ctx_out else w_down[:, MLA_Q_RANK:])
    kc_nope, kc_rope, vc = keys_values(dc[..., -(MLA_KV_RANK + MLA_ROPE):-MLA_ROPE], dc[..., -MLA_ROPE:])
    kc = join(kc_nope, kc_rope)
    q = jnp.concatenate([q_nope, apply_rope(q_rope, cos, sin)], axis=-1)[:, :, :, None, :]
    k = join(k_nope, apply_rope(k_rope, cos, sin))
    o = sweep_attention(q, jnp.concatenate([kc, k], axis=1), jnp.concatenate([vc, v], axis=1))
    out_l = o.reshape(B, S, MLA_HEADS * MLA_V) @ w_o
    if not ctx_out:
        return out_l, None
    L = uc.shape[1]
    qc_nope, qc_rope = queries(dc[..., :MLA_Q_RANK])
    qc = jnp.concatenate([qc_nope, qc_rope], axis=-1)[:, :, :, None, :]
    oc = sweep_attention(qc, kc, vc)
    out_c = oc.reshape(B, L, MLA_HEADS * MLA_V) @ w_o
    return out_l, out_c


def setup_inputs(seed: int = 0) -> dict:
    key = jax.random.key(seed)
    k = jax.random.split(key, 26)
    n_even = (DEPTH + 1) // 2
    n_odd = DEPTH // 2
    f32 = jnp.float32

    def dense(kk, shape, fan_in, mult=1.0):
        return jax.random.normal(kk, shape, f32) * (mult * fan_in ** -0.5)

    def gain(kk, shape):
        return 1.0 + 0.02 * jax.random.normal(kk, shape, f32)

    return {
        "x": jax.random.normal(k[0], (BATCH, SEQ, D_MODEL), f32),
        "c": jax.random.normal(k[1], (BATCH, D_MODEL), f32),
        "ctx": jax.random.normal(k[2], (BATCH, CTX_LEN, D_MODEL), f32),
        "c_ctx": jax.random.normal(k[3], (D_MODEL,), f32),
        "norm_g": gain(k[4], (DEPTH, 3, D_MODEL)),
        "w_mod": dense(k[5], (DEPTH, D_MODEL, N_MOD * D_MODEL), D_MODEL, 0.5),
        "b_mod": 0.02 * jax.random.normal(k[6], (DEPTH, N_MOD * D_MODEL), f32),
        "ffn_w1": dense(k[7], (DEPTH, 2, D_MODEL, 2 * D_FF), D_MODEL),
        "ffn_w2": dense(k[8], (DEPTH, 2, D_FF, D_MODEL), D_FF),
        "ev_w_in": dense(k[9], (n_even, D_MODEL, EV_Q_COLS + EV_KV_COLS), D_MODEL),
        "ev_w_out": dense(k[10], (n_even, NA_WIDTH + GQA_Q_WIDTH, D_MODEL), NA_WIDTH + GQA_Q_WIDTH),
        "na_q_g": gain(k[11], (n_even, HEAD_DIM)),
        "na_k_g": gain(k[12], (n_even, HEAD_DIM)),
        "na_rpb": 0.1 * jax.random.normal(k[13], (n_even, NA_HEADS, 2 * NA_WIN_H - 1, 2 * NA_WIN_W - 1), f32),
        "gq_q_g": gain(k[14], (n_even, HEAD_DIM)),
        "gq_k_g": gain(k[15], (n_even, HEAD_DIM)),
        "mla_w_down": dense(k[16], (n_odd, D_MODEL, MLA_Q_RANK + MLA_KV_RANK + MLA_ROPE), D_MODEL),
        "mla_q_a_g": gain(k[17], (n_odd, MLA_Q_RANK)),
        "mla_kv_a_g": gain(k[18], (n_odd, MLA_KV_RANK)),
        "mla_w_uq": dense(k[19], (n_odd, MLA_Q_RANK, MLA_HEADS * (MLA_NOPE + MLA_ROPE)), MLA_Q_RANK),
        "mla_w_ukv": dense(k[20], (n_odd, MLA_KV_RANK, MLA_HEADS * (MLA_NOPE + MLA_V)), MLA_KV_RANK),
        "mla_qn_g": gain(k[21], (n_odd, MLA_NOPE)),
        "mla_qr_g": gain(k[22], (n_odd, MLA_ROPE)),
        "mla_kn_g": gain(k[23], (n_odd, MLA_NOPE)),
        "mla_kr_g": gain(k[24], (n_odd, MLA_ROPE)),
        "mla_w_o": dense(k[25], (n_odd, MLA_HEADS * MLA_V, D_MODEL), MLA_HEADS * MLA_V),
    }


def reference(x, c, ctx, c_ctx, norm_g, w_mod, b_mod, ffn_w1, ffn_w2, ev_w_in, ev_w_out, na_q_g, na_k_g,
              na_rpb, gq_q_g, gq_k_g, mla_w_down, mla_q_a_g, mla_kv_a_g, mla_w_uq, mla_w_ukv, mla_qn_g,
              mla_qr_g, mla_kn_g, mla_kr_g, mla_w_o):
    S = x.shape[1]
    cos_h, sin_h = axial_angles(S, HEAD_DIM)
    cos_r, sin_r = axial_angles(S, MLA_ROPE)
    hl, hc = x, ctx
    for i in range(DEPTH):
        last = i == DEPTH - 1
        ml = [m[:, None, :] for m in jnp.split(jax.nn.silu(c) @ w_mod[i] + b_mod[i], N_MOD, axis=-1)]
        mc = jnp.split(jax.nn.silu(c_ctx) @ w_mod[i] + b_mod[i], N_MOD, axis=-1)
        g = norm_g[i]
        hl = hl + 0.5 * ml[2] * swiglu(modulate(hl, g[0], ml[0], ml[1]), ffn_w1[i, 0], ffn_w2[i, 0])
        hc = hc + 0.5 * mc[2] * swiglu(modulate(hc, g[0], mc[0], mc[1]), ffn_w1[i, 0], ffn_w2[i, 0])
        ul = modulate(hl, g[1], ml[3], ml[4])
        uc = modulate(hc, g[1], mc[3], mc[4])
        j = i // 2
        if i % 2 == 0:
            ol, oc = even_mixer(ul, uc, ev_w_in[j], ev_w_out[j], na_q_g[j], na_k_g[j], na_rpb[j],
                                gq_q_g[j], gq_k_g[j], cos_h, sin_h, not last)
        else:
            ol, oc = mla_mixer(ul, uc, mla_w_down[j], mla_q_a_g[j], mla_kv_a_g[j], mla_w_uq[j], mla_w_ukv[j],
                               mla_qn_g[j], mla_qr_g[j], mla_kn_g[j], mla_kr_g[j], mla_w_o[j],
                               cos_r, sin_r, not last)
        hl = hl + ml[5] * ol
        hl = hl + 0.5 * ml[8] * swiglu(modulate(hl, g[2], ml[6], ml[7]), ffn_w1[i, 1], ffn_w2[i, 1])
        if not last:
            hc = hc + mc[5] * oc
            hc = hc + 0.5 * mc[8] * swiglu(modulate(hc, g[2], mc[6], mc[7]), ffn_w1[i, 1], ffn_w2[i, 1])
    return hl
```

```python
import functools

import numpy as np
import jax
import jax.numpy as jnp
from jax import lax
from jax.experimental import pallas as pl
from jax.experimental.pallas import tpu as pltpu

D_MODEL = 4096
BATCH = 2
SEQ = 4096
DEPTH = 2
CTX_LEN = 256
GRID_W = 64
GRID_H = SEQ // GRID_W
HEAD_DIM = 128
EPS = 1e-6
ROPE_THETA = 10000.0
N_MOD = 9
D_FF = 10240
NA_HEADS = 16
NA_WIN_H = 8
NA_WIN_W = 16
GQA_Q_HEADS = 16
GQA_KV_HEADS = 4
GQA_GROUP = GQA_Q_HEADS // GQA_KV_HEADS
NA_WIDTH = NA_HEADS * HEAD_DIM
GQA_Q_WIDTH = GQA_Q_HEADS * HEAD_DIM
GQA_KV_WIDTH = GQA_KV_HEADS * HEAD_DIM
EV_Q_COLS = NA_WIDTH + GQA_Q_WIDTH
MLA_HEADS = 32
MLA_Q_RANK = 1024
MLA_KV_RANK = 512
MLA_NOPE = 128
MLA_ROPE = 64
MLA_V = 128
MLA_QK = MLA_NOPE + MLA_ROPE

LANES = 128
MLA_Q_SLOT = 2 * LANES
MLA_DOWN_COLS = MLA_Q_RANK + MLA_KV_RANK + LANES
MOD_ROWS = 8
CTX_MOD_ROW = BATCH

NA_GROUP_ROWS = 4
NA_BAND_ROWS = 12
NA_N_GROUPS = GRID_H // NA_GROUP_ROWS

VMEM_LIMIT = 56 * 1024 * 1024

BF = jnp.bfloat16
F32 = jnp.float32


def _params(n_axes):
    return pltpu.CompilerParams(dimension_semantics=("arbitrary",) * n_axes, vmem_limit_bytes=VMEM_LIMIT)


def _mod_kernel(c_ref, w_ref, b_ref, o_ref):
    c = c_ref[...]
    a = (c * jax.nn.sigmoid(c)).astype(BF)
    o_ref[...] = jnp.dot(a, w_ref[...].astype(BF), preferred_element_type=F32) + b_ref[...]


def _modulation(cvec, w_mod, b_mod):
    n = N_MOD * D_MODEL
    tn = 512
    return pl.pallas_call(
        _mod_kernel,
        grid=(DEPTH, n // tn),
        in_specs=[
            pl.BlockSpec((MOD_ROWS, D_MODEL), lambda l, j: (0, 0)),
            pl.BlockSpec((None, D_MODEL, tn), lambda l, j: (l, 0, j)),
            pl.BlockSpec((None, 1, tn), lambda l, j: (l, 0, j)),
        ],
        out_specs=pl.BlockSpec((None, MOD_ROWS, tn), lambda l, j: (l, 0, j)),
        out_shape=jax.ShapeDtypeStruct((DEPTH, MOD_ROWS, n), F32),
        compiler_params=_params(2),
        name="modulation",
    )(cvec, w_mod, b_mod.reshape(DEPTH, 1, n))


class _Tokens:
    def __init__(self, rows, tm, latent):
        self.rows, self.tm, self.latent = rows, tm, latent

    def mod_row(self, i):
        return i // (SEQ // self.tm) if self.latent else CTX_MOD_ROW

    def pos_block(self, i):
        return i % (SEQ // self.tm)


def _mod_spec(tok, layer, chunk, tn):
    return pl.BlockSpec((None, None, 1, tn),
                        lambda i, j: (layer, tok.mod_row(i), 0, chunk * (D_MODEL // tn) + j))


def _normmod_kernel(x_ref, g_ref, sh_ref, sc_ref, o_ref):
    x = x_ref[...]
    ms = jnp.mean(x * x, axis=-1, keepdims=True)
    y = x * lax.rsqrt(ms + EPS) * g_ref[...]
    o_ref[...] = (y * (1.0 + sc_ref[...]) + sh_ref[...]).astype(o_ref.dtype)


def _normmod(x, g, mt, layer, chunk, tok):
    tm = 256
    t = _Tokens(tok.rows, tm, tok.latent)
    sh = _mod_spec(t, layer, chunk, D_MODEL)
    sc = _mod_spec(t, layer, chunk + 1, D_MODEL)
    return pl.pallas_call(
        _normmod_kernel,
        grid=(tok.rows // tm,),
        in_specs=[
            pl.BlockSpec((tm, D_MODEL), lambda i: (i, 0)),
            pl.BlockSpec((1, D_MODEL), lambda i: (0, 0)),
            pl.BlockSpec(sh.block_shape, lambda i: sh.index_map(i, 0)),
            pl.BlockSpec(sc.block_shape, lambda i: sc.index_map(i, 0)),
        ],
        out_specs=pl.BlockSpec((tm, D_MODEL), lambda i: (i, 0)),
        out_shape=jax.ShapeDtypeStruct((tok.rows, D_MODEL), BF),
        compiler_params=_params(1),
        name="normmod",
    )(x, g.reshape(1, D_MODEL), mt, mt)


def _mm_kernel(*refs, n_rhs, n_extra, n_out, nk, epilogue):
    lhs_ref = refs[0]
    rhs_refs = refs[1:1 + n_rhs]
    extra = refs[1 + n_rhs:1 + n_rhs + n_extra]
    out_refs = refs[1 + n_rhs + n_extra:1 + n_rhs + n_extra + n_out]
    acc_refs = refs[1 + n_rhs + n_extra + n_out:]
    a = lhs_ref[...]
    if nk == 1:
        accs = [jnp.dot(a, r[...], preferred_element_type=F32) for r in rhs_refs]
        epilogue(accs, extra, out_refs)
        return
    k = pl.program_id(2)
    for r, acc in zip(rhs_refs, acc_refs):
        d = jnp.dot(a, r[...], preferred_element_type=F32)

        @pl.when(k == 0)
        def _():
            acc[...] = d

        @pl.when(k > 0)
        def _():
            acc[...] += d

    @pl.when(k == nk - 1)
    def _():
        epilogue([acc[...] for acc in acc_refs], extra, out_refs)


def _matmul(name, lhs, rhs, extras, outs, epilogue, *, rows, tm, tn, tk, n_cols):
    kdim = lhs.shape[1]
    nk = kdim // tk
    in_specs = [pl.BlockSpec((tm, tk), lambda i, j, k: (i, k))]
    for _, prefix, off in rhs:
        in_specs.append(pl.BlockSpec((None,) * len(prefix) + (tk, tn),
                                     lambda i, j, k, prefix=prefix, off=off: prefix + (k, off + j)))
    for _, spec in extras:
        in_specs.append(pl.BlockSpec(spec.block_shape, lambda i, j, k, spec=spec: spec.index_map(i, j)))
    out_specs = [pl.BlockSpec(spec.block_shape, lambda i, j, k, spec=spec: spec.index_map(i, j))
                 for _, _, spec in outs]
    scratch = [pltpu.VMEM((tm, tn), F32) for _ in rhs] if nk > 1 else []
    kern = functools.partial(_mm_kernel, n_rhs=len(rhs), n_extra=len(extras), n_out=len(outs), nk=nk,
                             epilogue=epilogue)
    res = pl.pallas_call(
        kern,
        grid=(rows // tm, n_cols // tn, nk),
        in_specs=in_specs,
        out_specs=out_specs,
        out_shape=[jax.ShapeDtypeStruct(shape, dtype) for shape, dtype, _ in outs],
        scratch_shapes=scratch,
        compiler_params=_params(3),
        name=name,
    )(lhs, *[r[0] for r in rhs], *[e[0] for e in extras])
    return res


def _epi_swiglu(accs, extra, outs):
    gate, up = accs
    outs[0][...] = (gate * jax.nn.sigmoid(gate) * up).astype(outs[0].dtype)


def _epi_residual(coef):
    def epi(accs, extra, outs):
        res_ref, gate_ref = extra
        outs[0][...] = res_ref[...] + (coef * gate_ref[...]) * accs[0]
    return epi


def _rms_chunk(x, gain, n_valid):
    ms = jnp.sum(x * x, axis=-1, keepdims=True) * (1.0 / n_valid)
    return x * lax.rsqrt(ms + EPS) * gain


def _rotate(x, tabs, shifts):
    y = x * tabs[0][...]
    for sh, t in zip(shifts, tabs[1:]):
        y = y + pltpu.roll(x, sh, 1) * t[...]
    return y


def _epi_heads(kinds, scale, shifts):
    def epi(accs, extra, outs):
        acc = accs[0]
        gain_ref, tabs = extra[0], extra[1:]
        for c, kind in enumerate(kinds):
            sl = slice(c * LANES, (c + 1) * LANES)
            x = acc[:, sl]
            if kind[0] != "P":
                x = _rms_chunk(x, gain_ref[:, sl], MLA_ROPE if kind[0] == "M" else LANES)
            if kind.endswith("R"):
                x = _rotate(x, tabs, shifts)
            if scale != 1.0:
                x = x * scale
            outs[0][:, sl] = x.astype(outs[0].dtype)
    return epi


def _epi_mla_down(rope):
    def epi(accs, extra, outs):
        acc = accs[0]
        gq_ref, gkv_ref, gkr_ref = extra[:3]
        tabs = extra[3:]
        cq = acc[:, :MLA_Q_RANK]
        outs[0][...] = _rms_chunk(cq, gq_ref[...], MLA_Q_RANK).astype(outs[0].dtype)
        ckv = acc[:, MLA_Q_RANK:MLA_Q_RANK + MLA_KV_RANK]
        outs[1][...] = _rms_chunk(ckv, gkv_ref[...], MLA_KV_RANK).astype(outs[1].dtype)
        kr = _rms_chunk(acc[:, MLA_Q_RANK + MLA_KV_RANK:], gkr_ref[...], MLA_ROPE)
        if rope:
            kr = _rotate(kr, tabs, _ROPE64_SHIFTS)
        outs[2][...] = kr.astype(outs[2].dtype)
    return epi


def _axial_angles(rot_dim):
    t = jnp.arange(SEQ)
    row = (t // GRID_W).astype(F32)
    col = (t % GRID_W).astype(F32)
    axis_dim = rot_dim // 2
    inv_freq = ROPE_THETA ** (-jnp.arange(0, axis_dim, 2, dtype=F32) / axis_dim)
    ang = jnp.concatenate([row[:, None] * inv_freq, col[:, None] * inv_freq], axis=-1)
    return jnp.cos(ang), jnp.sin(ang)


_ROPE128_SHIFTS = (HEAD_DIM // 2,)
_ROPE64_SHIFTS = (LANES - MLA_ROPE // 2, MLA_ROPE // 2)


def _rope128_tables():
    cos, sin = _axial_angles(HEAD_DIM)
    return [jnp.concatenate([cos, cos], -1), jnp.concatenate([-sin, sin], -1)]


def _rope64_tables():
    cos, sin = _axial_angles(MLA_ROPE)
    z32 = jnp.zeros_like(sin)
    z64 = jnp.zeros((SEQ, LANES - MLA_ROPE), F32)
    return [jnp.concatenate([cos, cos, z64], -1),
            jnp.concatenate([-sin, z32, z64], -1),
            jnp.concatenate([z32, sin, z64], -1)]


def _rope_extras(tabs, tok):
    return [(t, pl.BlockSpec((tok.tm, LANES), lambda i, j: (tok.pos_block(i), 0))) for t in tabs]


def _attn_kernel(*refs, n_seg, n_parts, group, dk, dv, tq, seg_lens):
    q_ref = refs[0]
    k_refs = refs[1:1 + n_seg * n_parts]
    v_refs = refs[1 + n_seg * n_parts:1 + n_seg * n_parts + n_seg]
    o_ref, k_scr, v_scr = refs[1 + n_seg * n_parts + n_seg:]

    @pl.when(pl.program_id(2) == 0)
    def _():
        off = 0
        for s in range(n_seg):
            length = seg_lens[s]
            col = 0
            for part in k_refs[s * n_parts:(s + 1) * n_parts]:
                width = part.shape[-1]
                k_scr[off:off + length, col:col + width] = part[...]
                col += width
            v_scr[off:off + length, :] = v_refs[s][...]
            off += length

    q = q_ref[...]
    if group > 1:
        q = jnp.concatenate([q[:, g * dk:(g + 1) * dk] for g in range(group)], axis=0)
    s = lax.dot_general(q, k_scr[...], (((1,), (1,)), ((), ())), preferred_element_type=F32)
    m = jnp.max(s, axis=-1, keepdims=True)
    p = jnp.exp(s - m)
    l = jnp.sum(p, axis=-1, keepdims=True)
    o = jnp.dot(p.astype(BF), v_scr[...], preferred_element_type=F32) / l
    for g in range(group):
        o_ref[:, g * dv:(g + 1) * dv] = o[g * tq:(g + 1) * tq].astype(o_ref.dtype)


def _attention(name, q, segs, *, q_rows, n_kv, group, dk, dv, tq):
    nq = q_rows // tq
    n_parts = len(segs[0][1])
    in_specs = [pl.BlockSpec((tq, group * dk), lambda b, h, i: (b * nq + i, h))]
    operands = [q]
    for length, k_parts, _ in segs:
        for arr, col in k_parts:
            in_specs.append(pl.BlockSpec((length, LANES), lambda b, h, i, col=col: (b, col(h))))
            operands.append(arr)
    for length, _, (arr, col) in segs:
        in_specs.append(pl.BlockSpec((length, dv), lambda b, h, i, col=col: (b, col(h))))
        operands.append(arr)
    total = sum(s[0] for s in segs)
    kern = functools.partial(_attn_kernel, n_seg=len(segs), n_parts=n_parts, group=group, dk=dk, dv=dv,
                             tq=tq, seg_lens=tuple(s[0] for s in segs))
    return pl.pallas_call(
        kern,
        grid=(BATCH, n_kv, nq),
        in_specs=in_specs,
        out_specs=pl.BlockSpec((tq, group * dv), lambda b, h, i: (b * nq + i, h)),
        out_shape=jax.ShapeDtypeStruct((BATCH * q_rows, n_kv * group * dv), BF),
        scratch_shapes=[pltpu.VMEM((total, dk), BF), pltpu.VMEM((total, dv), BF)],
        compiler_params=_params(3),
        name=name,
    )(*operands)


def _na_band_start(g):
    return jnp.clip(NA_GROUP_ROWS * g - NA_WIN_H // 2, 0, GRID_H - NA_BAND_ROWS)


def _na_kernel(q_ref, k_ref, v_ref, kc_ref, vc_ref, bias_ref, o_ref):
    start = pl.multiple_of(_na_band_start(pl.program_id(2)) * GRID_W, GRID_W)
    band = NA_BAND_ROWS * GRID_W
    kb = k_ref[pl.ds(start, band), :]
    vb = v_ref[pl.ds(start, band), :]
    q = q_ref[...]
    nt = (((1,), (1,)), ((), ()))
    s_b = lax.dot_general(q, kb, nt, preferred_element_type=F32) + bias_ref[...]
    s_c = lax.dot_general(q, kc_ref[...], nt, preferred_element_type=F32)
    m = jnp.maximum(jnp.max(s_b, axis=-1, keepdims=True), jnp.max(s_c, axis=-1, keepdims=True))
    p_b = jnp.exp(s_b - m)
    p_c = jnp.exp(s_c - m)
    l = jnp.sum(p_b, axis=-1, keepdims=True) + jnp.sum(p_c, axis=-1, keepdims=True)
    o = (jnp.dot(p_b.astype(BF), vb, preferred_element_type=F32)
         + jnp.dot(p_c.astype(BF), vc_ref[...], preferred_element_type=F32))
    o_ref[...] = (o / l).astype(o_ref.dtype)


def _na_bias_tables(rpb):
    ri = np.arange(NA_GROUP_ROWS)
    bi = np.arange(NA_BAND_ROWS)
    col = np.arange(GRID_W)
    c0 = np.clip(col - NA_WIN_W // 2, 0, GRID_W - NA_WIN_W)
    col_ok = (col[None, :] >= c0[:, None]) & (col[None, :] < c0[:, None] + NA_WIN_W)
    dc = np.clip(col[None, :] - col[:, None] + NA_WIN_W - 1, 0, 2 * NA_WIN_W - 2)
    row_idx, col_idx, ok = [], [], []
    for g in (0, 1, NA_N_GROUPS - 1):
        r = NA_GROUP_ROWS * g + ri
        r0 = np.clip(r - NA_WIN_H // 2, 0, GRID_H - NA_WIN_H)
        start = min(int(r0[0]), GRID_H - NA_BAND_ROWS)
        kr = start + bi
        row_ok = (kr[None, :] >= r0[:, None]) & (kr[None, :] < r0[:, None] + NA_WIN_H)
        dr = np.clip(kr[None, :] - r[:, None] + NA_WIN_H - 1, 0, 2 * NA_WIN_H - 2)
        shape = (NA_GROUP_ROWS, GRID_W, NA_BAND_ROWS, GRID_W)
        flat = (NA_GROUP_ROWS * GRID_W, NA_BAND_ROWS * GRID_W)
        row_idx.append(np.broadcast_to(dr[:, None, :, None], shape).reshape(flat))
        col_idx.append(np.broadcast_to(dc[None, :, None, :], shape).reshape(flat))
        ok.append(np.broadcast_to(row_ok[:, None, :, None] & col_ok[None, :, None, :], shape).reshape(flat))
    row_idx, col_idx, ok = np.stack(row_idx), np.stack(col_idx), np.stack(ok)
    return jnp.where(ok[None], rpb[:, row_idx, col_idx], -jnp.inf)


def _na_attention(q, k, v, kc, vc, bias):
    tq = NA_GROUP_ROWS * GRID_W
    band = NA_BAND_ROWS * GRID_W

    def variant(g):
        return jnp.minimum(g, 1) + (g == NA_N_GROUPS - 1).astype(jnp.int32)

    head_block = lambda b, h, g: (b, h)
    return pl.pallas_call(
        _na_kernel,
        grid=(BATCH, NA_HEADS, NA_N_GROUPS),
        in_specs=[
            pl.BlockSpec((tq, HEAD_DIM), lambda b, h, g: (b * NA_N_GROUPS + g, h)),
            pl.BlockSpec((SEQ, HEAD_DIM), head_block),
            pl.BlockSpec((SEQ, HEAD_DIM), head_block),
            pl.BlockSpec((CTX_LEN, HEAD_DIM), head_block),
            pl.BlockSpec((CTX_LEN, HEAD_DIM), head_block),
            pl.BlockSpec((None, None, tq, band), lambda b, h, g: (h, variant(g), 0, 0)),
        ],
        out_specs=pl.BlockSpec((tq, HEAD_DIM), lambda b, h, g: (b * NA_N_GROUPS + g, h)),
        out_shape=jax.ShapeDtypeStruct((BATCH * SEQ, NA_WIDTH), BF),
        compiler_params=_params(3),
        name="na_attention",
    )(q, k, v, kc, vc, bias)


def _head_gain(g, width):
    return jnp.tile(g, width // g.shape[0]).reshape(1, width)


def _ffn(h, g, mt, layer, f, w1, w2, tok):
    chunk = 6 * f
    hn = _normmod(h, g, mt, layer, chunk, tok)
    tn1 = 512
    u, = _matmul(
        "ffn_up", hn, [(w1, (layer, f), 0), (w1, (layer, f), D_FF // tn1)], [],
        [((tok.rows, D_FF), BF, pl.BlockSpec((tok.tm, tn1), lambda i, j: (i, j)))],
        _epi_swiglu, rows=tok.rows, tm=tok.tm, tn=tn1, tk=D_MODEL, n_cols=D_FF)
    tn2 = 1024
    out, = _matmul(
        "ffn_down", u, [(w2, (layer, f), 0)],
        [(h, pl.BlockSpec((tok.tm, tn2), lambda i, j: (i, j))), (mt, _mod_spec(tok, layer, chunk + 2, tn2))],
        [((tok.rows, D_MODEL), F32, pl.BlockSpec((tok.tm, tn2), lambda i, j: (i, j)))],
        _epi_residual(0.5), rows=tok.rows, tm=tok.tm, tn=tn2, tk=2048, n_cols=D_MODEL)
    return out


def _out_proj(name, o, w, prefix, h, mt, layer, tok):
    tn = 512
    out, = _matmul(
        name, o, [(w, prefix, 0)],
        [(h, pl.BlockSpec((tok.tm, tn), lambda i, j: (i, j))), (mt, _mod_spec(tok, layer, 5, tn))],
        [((tok.rows, D_MODEL), F32, pl.BlockSpec((tok.tm, tn), lambda i, j: (i, j)))],
        _epi_residual(1.0), rows=tok.rows, tm=tok.tm, tn=tn, tk=o.shape[1], n_cols=D_MODEL)
    return out


def _head_proj(name, u, w, prefix, col_off, width, kinds, gain, scale, tabs, shifts, tok, tn=512):
    out, = _matmul(
        name, u, [(w, prefix, col_off // tn)],
        [(gain, pl.BlockSpec((1, tn), lambda i, j: (0, j)))] + _rope_extras(tabs, tok),
        [((tok.rows, width), BF, pl.BlockSpec((tok.tm, tn), lambda i, j: (i, j)))],
        _epi_heads(kinds * (tn // (LANES * len(kinds))), scale, shifts),
        rows=tok.rows, tm=tok.tm, tn=tn, tk=u.shape[1], n_cols=width)
    return out


def _even_projections(u, w_in, j, gains, tok, want_q):
    na_q_g, na_k_g, gq_q_g, gq_k_g = gains
    scale = HEAD_DIM ** -0.5
    rot = "NR" if tok.latent else "N"
    tabs = _rope128_tables() if tok.latent else []
    ones = jnp.ones((HEAD_DIM,), F32)
    proj = functools.partial(_head_proj, u=u, w=w_in, prefix=(j,), tok=tok)
    out = {}
    if want_q:
        out["na_q"] = proj("na_q", col_off=0, width=NA_WIDTH, kinds=("N",),
                           gain=_head_gain(na_q_g, NA_WIDTH), scale=scale, tabs=[], shifts=())
        out["gq_q"] = proj("gq_q", col_off=NA_WIDTH, width=GQA_Q_WIDTH, kinds=(rot,),
                           gain=_head_gain(gq_q_g, GQA_Q_WIDTH), scale=scale, tabs=tabs, shifts=_ROPE128_SHIFTS)
    out["na_k"] = proj("na_k", col_off=EV_Q_COLS, width=NA_WIDTH, kinds=("N",),
                       gain=_head_gain(na_k_g, NA_WIDTH), scale=1.0, tabs=[], shifts=())
    out["na_v"] = proj("na_v", col_off=EV_Q_COLS + NA_WIDTH, width=NA_WIDTH, kinds=("P",),
                       gain=_head_gain(ones, NA_WIDTH), scale=1.0, tabs=[], shifts=())
    out["gq_k"] = proj("gq_k", col_off=EV_Q_COLS + 2 * NA_WIDTH, width=GQA_KV_WIDTH, kinds=(rot,),
                       gain=_head_gain(gq_k_g, GQA_KV_WIDTH), scale=1.0, tabs=tabs, shifts=_ROPE128_SHIFTS)
    out["gq_v"] = proj("gq_v", col_off=EV_Q_COLS + 2 * NA_WIDTH + GQA_KV_WIDTH, width=GQA_KV_WIDTH,
                       kinds=("P",), gain=_head_gain(ones, GQA_KV_WIDTH), scale=1.0, tabs=[], shifts=())
    return out


def _even_mixer(ul, uc, hl, hc, mt, layer, w_in, w_out, j, gains, rpb, lat, ctx, ctx_out):
    pl_ = _even_projections(ul, w_in, j, gains, lat, True)
    pc = _even_projections(uc, w_in, j, gains, ctx, ctx_out)
    head = lambda h: h
    a = _na_attention(pl_["na_q"], pl_["na_k"], pl_["na_v"], pc["na_k"], pc["na_v"], _na_bias_tables(rpb))
    gq_segs = [(CTX_LEN, [(pc["gq_k"], head)], (pc["gq_v"], head)),
               (SEQ, [(pl_["gq_k"], head)], (pl_["gq_v"], head))]
    b = _attention("gqa", pl_["gq_q"], gq_segs, q_rows=SEQ, n_kv=GQA_KV_HEADS, group=GQA_GROUP,
                   dk=HEAD_DIM, dv=HEAD_DIM, tq=128)
    hl = _out_proj("ev_out", jnp.concatenate([a, b], axis=-1), w_out, (j,), hl, mt, layer, lat)
    if not ctx_out:
        return hl, hc
    ac = _attention("na_ctx", pc["na_q"], [(CTX_LEN, [(pc["na_k"], head)], (pc["na_v"], head))],
                    q_rows=CTX_LEN, n_kv=NA_HEADS, group=1, dk=HEAD_DIM, dv=HEAD_DIM, tq=CTX_LEN)
    bc = _attention("gqa_ctx", pc["gq_q"], gq_segs[:1], q_rows=CTX_LEN, n_kv=GQA_KV_HEADS, group=GQA_GROUP,
                    dk=HEAD_DIM, dv=HEAD_DIM, tq=128)
    hc = _out_proj("ev_out_ctx", jnp.concatenate([ac, bc], axis=-1), w_out, (j,), hc, mt, layer, ctx)
    return hl, hc


def _mla_down(u, w_down_p, gains, tok):
    q_a_g, kv_a_g, kr_g = gains
    tabs = _rope64_tables() if tok.latent else []
    tm = 512
    t = _Tokens(tok.rows, tm, tok.latent)
    full = lambda n: pl.BlockSpec((1, n), lambda i, j: (0, 0))
    rows = lambda n: pl.BlockSpec((tm, n), lambda i, j: (i, 0))
    return _matmul(
        "mla_down", u, [(w_down_p, (), 0)],
        [(q_a_g.reshape(1, -1), full(MLA_Q_RANK)), (kv_a_g.reshape(1, -1), full(MLA_KV_RANK)),
         (kr_g, full(LANES))] + _rope_extras(tabs, t),
        [((tok.rows, MLA_Q_RANK), BF, rows(MLA_Q_RANK)), ((tok.rows, MLA_KV_RANK), BF, rows(MLA_KV_RANK)),
         ((tok.rows, LANES), BF, rows(LANES))],
        _epi_mla_down(tok.latent), rows=tok.rows, tm=tm, tn=MLA_DOWN_COLS, tk=D_MODEL, n_cols=MLA_DOWN_COLS)


def _mla_mixer(ul, uc, hl, hc, mt, layer, w, g, lat, ctx, ctx_out):
    pad64 = lambda v: jnp.concatenate([v, jnp.zeros((LANES - MLA_ROPE,), F32)])
    kr_gain = pad64(g["kr"]).reshape(1, LANES)
    q_gain = _head_gain(jnp.concatenate([g["qn"], pad64(g["qr"])]), MLA_HEADS * MLA_Q_SLOT)
    kv_gain = _head_gain(jnp.concatenate([g["kn"], jnp.ones((MLA_V,), F32)]), MLA_HEADS * (MLA_NOPE + MLA_V))
    down_gains = (g["q_a"], g["kv_a"], kr_gain)
    tabs = _rope64_tables()

    def keys_values(ckv, tok):
        return _head_proj("mla_kv", ckv, w["ukv"], (), 0, MLA_HEADS * (MLA_NOPE + MLA_V), ("N", "P"),
                          kv_gain, 1.0, [], (), tok)

    cq_l, ckv_l, kr_l = _mla_down(ul, w["down"], down_gains, lat)
    cq_c, ckv_c, kr_c = _mla_down(uc, w["down"], down_gains, ctx)
    kv_l = keys_values(ckv_l, lat)
    kv_c = keys_values(ckv_c, ctx)
    scale = MLA_QK ** -0.5
    q_l = _head_proj("mla_q", cq_l, w["uq"], (), 0, MLA_HEADS * MLA_Q_SLOT, ("N", "MR"), q_gain, scale,
                     tabs, _ROPE64_SHIFTS, lat)
    nope = lambda h: 2 * h
    val = lambda h: 2 * h + 1
    shared = lambda h: 0
    segs = [(CTX_LEN, [(kv_c, nope), (kr_c, shared)], (kv_c, val)),
            (SEQ, [(kv_l, nope), (kr_l, shared)], (kv_l, val))]
    o = _attention("mla", q_l, segs, q_rows=SEQ, n_kv=MLA_HEADS, group=1, dk=MLA_Q_SLOT, dv=MLA_V, tq=512)
    hl = _out_proj("mla_out", o, w["o"], (), hl, mt, layer, lat)
    if not ctx_out:
        return hl, hc
    q_c = _head_proj("mla_q_ctx", cq_c, w["uq"], (), 0, MLA_HEADS * MLA_Q_SLOT, ("N", "M"), q_gain, scale,
                     [], (), ctx)
    oc = _attention("mla_ctx", q_c, segs[:1], q_rows=CTX_LEN, n_kv=MLA_HEADS, group=1, dk=MLA_Q_SLOT,
                    dv=MLA_V, tq=CTX_LEN)
    hc = _out_proj("mla_out_ctx", oc, w["o"], (), hc, mt, layer, ctx)
    return hl, hc


def _mla_weights(w_down, w_uq, w_ukv, w_o, j):
    down = jnp.pad(w_down[j].astype(BF), ((0, 0), (0, LANES - MLA_ROPE)))
    uq = w_uq[j].astype(BF).reshape(MLA_Q_RANK, MLA_HEADS, MLA_QK)
    uq = jnp.pad(uq, ((0, 0), (0, 0), (0, MLA_Q_SLOT - MLA_QK))).reshape(MLA_Q_RANK, MLA_HEADS * MLA_Q_SLOT)
    return {"down": down, "uq": uq, "ukv": w_ukv[j].astype(BF), "o": w_o[j].astype(BF)}


def kernel(x, c, ctx, c_ctx, norm_g, w_mod, b_mod, ffn_w1, ffn_w2, ev_w_in, ev_w_out, na_q_g, na_k_g, na_rpb, gq_q_g, gq_k_g, mla_w_down, mla_q_a_g, mla_kv_a_g, mla_w_uq, mla_w_ukv, mla_qn_g, mla_qr_g, mla_kn_g, mla_kr_g, mla_w_o):
    lat = _Tokens(BATCH * SEQ, 1024, True)
    ctk = _Tokens(BATCH * CTX_LEN, 512, False)
    cvec = jnp.zeros((MOD_ROWS, D_MODEL), F32).at[:BATCH].set(c).at[CTX_MOD_ROW].set(c_ctx)
    mt = _modulation(cvec, w_mod, b_mod).reshape(DEPTH, MOD_ROWS, 1, N_MOD * D_MODEL)
    w1 = ffn_w1.astype(BF)
    w2 = ffn_w2.astype(BF)
    w_in = ev_w_in.astype(BF)
    w_out = ev_w_out.astype(BF)
    hl = x.reshape(BATCH * SEQ, D_MODEL)
    hc = ctx.reshape(BATCH * CTX_LEN, D_MODEL)
    for i in range(DEPTH):
        last = i == DEPTH - 1
        g = norm_g[i]
        hl = _ffn(hl, g[0], mt, i, 0, w1, w2, lat)
        hc = _ffn(hc, g[0], mt, i, 0, w1, w2, ctk)
        ul = _normmod(hl, g[1], mt, i, 3, lat)
        uc = _normmod(hc, g[1], mt, i, 3, ctk)
        j = i // 2
        if i % 2 == 0:
            gains = (na_q_g[j], na_k_g[j], gq_q_g[j], gq_k_g[j])
            hl, hc = _even_mixer(ul, uc, hl, hc, mt, i, w_in, w_out, j, gains, na_rpb[j], lat, ctk, not last)
        else:
            w = _mla_weights(mla_w_down, mla_w_uq, mla_w_ukv, mla_w_o, j)
            gains = {"q_a": mla_q_a_g[j], "kv_a": mla_kv_a_g[j], "qn": mla_qn_g[j], "qr": mla_qr_g[j],
                     "kn": mla_kn_g[j], "kr": mla_kr_g[j]}
            hl, hc = _mla_mixer(ul, uc, hl, hc, mt, i, w, gains, lat, ctk, not last)
        hl = _ffn(hl, g[2], mt, i, 1, w1, w2, lat)
        if not last:
            hc = _ffn(hc, g[2], mt, i, 1, w1, w2, ctk)
    return hl.reshape(BATCH, SEQ, D_MODEL)
```

```python
import functools

import numpy as np
import jax
import jax.numpy as jnp
from jax import lax
from jax.experimental import pallas as pl
from jax.experimental.pallas import tpu as pltpu

D_MODEL = 4096
BATCH = 2
SEQ = 4096
DEPTH = 2
CTX_LEN = 256
GRID_W = 64
GRID_H = SEQ // GRID_W
HEAD_DIM = 128
EPS = 1e-6
ROPE_THETA = 10000.0
N_MOD = 9
D_FF = 10240
NA_HEADS = 16
NA_WIN_H = 8
NA_WIN_W = 16
GQA_Q_HEADS = 16
GQA_KV_HEADS = 4
GQA_GROUP = GQA_Q_HEADS // GQA_KV_HEADS
NA_WIDTH = NA_HEADS * HEAD_DIM
GQA_Q_WIDTH = GQA_Q_HEADS * HEAD_DIM
GQA_KV_WIDTH = GQA_KV_HEADS * HEAD_DIM
EV_Q_COLS = NA_WIDTH + GQA_Q_WIDTH
MLA_HEADS = 32
MLA_Q_RANK = 1024
MLA_KV_RANK = 512
MLA_NOPE = 128
MLA_ROPE = 64
MLA_V = 128
MLA_QK = MLA_NOPE + MLA_ROPE

LANES = 128
MLA_Q_SLOT = 2 * LANES
MLA_DOWN_COLS = MLA_Q_RANK + MLA_KV_RANK + LANES
MOD_ROWS = 8
CTX_MOD_ROW = BATCH

NA_GROUP_ROWS = 4
NA_BAND_ROWS = 12
NA_N_GROUPS = GRID_H // NA_GROUP_ROWS

ATTN_KEY_CHUNK = 256
LOG2E = 1.4426950408889634

VMEM_LIMIT = 56 * 1024 * 1024

BF = jnp.bfloat16
F32 = jnp.float32


def _params(n_axes):
    return pltpu.CompilerParams(dimension_semantics=("arbitrary",) * n_axes, vmem_limit_bytes=VMEM_LIMIT)


def _mod_kernel(c_ref, w_ref, b_ref, o_ref):
    c = c_ref[...]
    a = (c * jax.nn.sigmoid(c)).astype(BF)
    o_ref[...] = jnp.dot(a, w_ref[...].astype(BF), preferred_element_type=F32) + b_ref[...]


def _modulation(cvec, w_mod, b_mod):
    n = N_MOD * D_MODEL
    tn = 512
    return pl.pallas_call(
        _mod_kernel,
        grid=(DEPTH, n // tn),
        in_specs=[
            pl.BlockSpec((MOD_ROWS, D_MODEL), lambda l, j: (0, 0)),
            pl.BlockSpec((None, D_MODEL, tn), lambda l, j: (l, 0, j)),
            pl.BlockSpec((None, 1, tn), lambda l, j: (l, 0, j)),
        ],
        out_specs=pl.BlockSpec((None, MOD_ROWS, tn), lambda l, j: (l, 0, j)),
        out_shape=jax.ShapeDtypeStruct((DEPTH, MOD_ROWS, n), F32),
        compiler_params=_params(2),
        name="modulation",
    )(cvec, w_mod, b_mod.reshape(DEPTH, 1, n))


class _Tokens:
    def __init__(self, rows, tm, latent):
        self.rows, self.tm, self.latent = rows, tm, latent

    def mod_row(self, i):
        return i // (SEQ // self.tm) if self.latent else CTX_MOD_ROW

    def pos_block(self, i):
        return i % (SEQ // self.tm)


def _mod_spec(tok, layer, chunk, tn):
    return pl.BlockSpec((None, None, 1, tn),
                        lambda i, j: (layer, tok.mod_row(i), 0, chunk * (D_MODEL // tn) + j))


def _normmod_kernel(x_ref, g_ref, sh_ref, sc_ref, o_ref):
    x = x_ref[...]
    ms = jnp.mean(x * x, axis=-1, keepdims=True)
    y = x * lax.rsqrt(ms + EPS) * g_ref[...]
    o_ref[...] = (y * (1.0 + sc_ref[...]) + sh_ref[...]).astype(o_ref.dtype)


def _normmod(x, g, mt, layer, chunk, tok):
    tm = 256
    t = _Tokens(tok.rows, tm, tok.latent)
    sh = _mod_spec(t, layer, chunk, D_MODEL)
    sc = _mod_spec(t, layer, chunk + 1, D_MODEL)
    return pl.pallas_call(
        _normmod_kernel,
        grid=(tok.rows // tm,),
        in_specs=[
            pl.BlockSpec((tm, D_MODEL), lambda i: (i, 0)),
            pl.BlockSpec((1, D_MODEL), lambda i: (0, 0)),
            pl.BlockSpec(sh.block_shape, lambda i: sh.index_map(i, 0)),
            pl.BlockSpec(sc.block_shape, lambda i: sc.index_map(i, 0)),
        ],
        out_specs=pl.BlockSpec((tm, D_MODEL), lambda i: (i, 0)),
        out_shape=jax.ShapeDtypeStruct((tok.rows, D_MODEL), BF),
        compiler_params=_params(1),
        name="normmod",
    )(x, g.reshape(1, D_MODEL), mt, mt)


def _mm_kernel(*refs, n_rhs, n_extra, n_out, nk, epilogue):
    lhs_ref = refs[0]
    rhs_refs = refs[1:1 + n_rhs]
    extra = refs[1 + n_rhs:1 + n_rhs + n_extra]
    out_refs = refs[1 + n_rhs + n_extra:1 + n_rhs + n_extra + n_out]
    acc_refs = refs[1 + n_rhs + n_extra + n_out:]
    a = lhs_ref[...]
    if nk == 1:
        accs = [jnp.dot(a, r[...], preferred_element_type=F32) for r in rhs_refs]
        epilogue(accs, extra, out_refs)
        return
    k = pl.program_id(2)
    for r, acc in zip(rhs_refs, acc_refs):
        d = jnp.dot(a, r[...], preferred_element_type=F32)

        @pl.when(k == 0)
        def _():
            acc[...] = d

        @pl.when(k > 0)
        def _():
            acc[...] += d

    @pl.when(k == nk - 1)
    def _():
        epilogue([acc[...] for acc in acc_refs], extra, out_refs)


def _matmul(name, lhs, rhs, extras, outs, epilogue, *, rows, tm, tn, tk, n_cols, lhs_single=False):
    kdim = lhs.shape[1]
    nk = kdim // tk
    lhs_mode = pl.Buffered(1) if lhs_single else None
    in_specs = [pl.BlockSpec((tm, tk), lambda i, j, k: (i, k), pipeline_mode=lhs_mode)]
    for _, prefix, off in rhs:
        in_specs.append(pl.BlockSpec((None,) * len(prefix) + (tk, tn),
                                     lambda i, j, k, prefix=prefix, off=off: prefix + (k, off + j)))
    for _, spec in extras:
        in_specs.append(pl.BlockSpec(spec.block_shape, lambda i, j, k, spec=spec: spec.index_map(i, j)))
    out_specs = [pl.BlockSpec(spec.block_shape, lambda i, j, k, spec=spec: spec.index_map(i, j))
                 for _, _, spec in outs]
    scratch = [pltpu.VMEM((tm, tn), F32) for _ in rhs] if nk > 1 else []
    kern = functools.partial(_mm_kernel, n_rhs=len(rhs), n_extra=len(extras), n_out=len(outs), nk=nk,
                             epilogue=epilogue)
    res = pl.pallas_call(
        kern,
        grid=(rows // tm, n_cols // tn, nk),
        in_specs=in_specs,
        out_specs=out_specs,
        out_shape=[jax.ShapeDtypeStruct(shape, dtype) for shape, dtype, _ in outs],
        scratch_shapes=scratch,
        compiler_params=_params(3),
        name=name,
    )(lhs, *[r[0] for r in rhs], *[e[0] for e in extras])
    return res


def _epi_swiglu(accs, extra, outs):
    gate, up = accs
    outs[0][...] = (gate * jax.nn.sigmoid(gate) * up).astype(outs[0].dtype)


def _epi_residual(coef):
    def epi(accs, extra, outs):
        res_ref, gate_ref = extra
        outs[0][...] = res_ref[...] + (coef * gate_ref[...]) * accs[0]
    return epi


def _rms_chunk(x, gain, n_valid):
    ms = jnp.sum(x * x, axis=-1, keepdims=True) * (1.0 / n_valid)
    return x * lax.rsqrt(ms + EPS) * gain


def _rotate(x, tabs, shifts):
    y = x * tabs[0][...]
    for sh, t in zip(shifts, tabs[1:]):
        y = y + pltpu.roll(x, sh, 1) * t[...]
    return y


def _epi_heads(kinds, scale, shifts):
    def epi(accs, extra, outs):
        acc = accs[0]
        gain_ref, tabs = extra[0], extra[1:]
        for c, kind in enumerate(kinds):
            sl = slice(c * LANES, (c + 1) * LANES)
            x = acc[:, sl]
            if kind[0] != "P":
                x = _rms_chunk(x, gain_ref[:, sl], MLA_ROPE if kind[0] == "M" else LANES)
            if kind.endswith("R"):
                x = _rotate(x, tabs, shifts)
            if scale != 1.0:
                x = x * scale
            outs[0][:, sl] = x.astype(outs[0].dtype)
    return epi


def _epi_mla_down(rope):
    def epi(accs, extra, outs):
        acc = accs[0]
        gq_ref, gkv_ref, gkr_ref = extra[:3]
        tabs = extra[3:]
        cq = acc[:, :MLA_Q_RANK]
        outs[0][...] = _rms_chunk(cq, gq_ref[...], MLA_Q_RANK).astype(outs[0].dtype)
        ckv = acc[:, MLA_Q_RANK:MLA_Q_RANK + MLA_KV_RANK]
        outs[1][...] = _rms_chunk(ckv, gkv_ref[...], MLA_KV_RANK).astype(outs[1].dtype)
        kr = _rms_chunk(acc[:, MLA_Q_RANK + MLA_KV_RANK:], gkr_ref[...], MLA_ROPE)
        if rope:
            kr = _rotate(kr, tabs, _ROPE64_SHIFTS)
        outs[2][...] = kr.astype(outs[2].dtype)
    return epi


def _axial_angles(rot_dim):
    t = jnp.arange(SEQ)
    row = (t // GRID_W).astype(F32)
    col = (t % GRID_W).astype(F32)
    axis_dim = rot_dim // 2
    inv_freq = ROPE_THETA ** (-jnp.arange(0, axis_dim, 2, dtype=F32) / axis_dim)
    ang = jnp.concatenate([row[:, None] * inv_freq, col[:, None] * inv_freq], axis=-1)
    return jnp.cos(ang), jnp.sin(ang)


_ROPE128_SHIFTS = (HEAD_DIM // 2,)
_ROPE64_SHIFTS = (LANES - MLA_ROPE // 2, MLA_ROPE // 2)


def _rope128_tables():
    cos, sin = _axial_angles(HEAD_DIM)
    return [jnp.concatenate([cos, cos], -1), jnp.concatenate([-sin, sin], -1)]


def _rope64_tables():
    cos, sin = _axial_angles(MLA_ROPE)
    z32 = jnp.zeros_like(sin)
    z64 = jnp.zeros((SEQ, LANES - MLA_ROPE), F32)
    return [jnp.concatenate([cos, cos, z64], -1),
            jnp.concatenate([-sin, z32, z64], -1),
            jnp.concatenate([z32, sin, z64], -1)]


def _rope_extras(tabs, tok):
    return [(t, pl.BlockSpec((tok.tm, LANES), lambda i, j: (tok.pos_block(i), 0))) for t in tabs]


def _attn_kernel(*refs, n_seg, n_parts, group, dk, dv, tq, seg_lens, chunk):
    q_ref = refs[0]
    k_refs = refs[1:1 + n_seg * n_parts]
    v_refs = refs[1 + n_seg * n_parts:1 + n_seg * n_parts + n_seg]
    o_ref, k_scr, v_scr = refs[1 + n_seg * n_parts + n_seg:]

    @pl.when(pl.program_id(2) == 0)
    def _():
        off = 0
        for s in range(n_seg):
            length = seg_lens[s]
            col = 0
            for part in k_refs[s * n_parts:(s + 1) * n_parts]:
                width = part.shape[-1]
                k_scr[off:off + length, col:col + width] = part[...]
                col += width
            v_scr[off:off + length, :] = v_refs[s][...]
            off += length

    q = q_ref[...]
    if group > 1:
        q = jnp.concatenate([q[:, g * dk:(g + 1) * dk] for g in range(group)], axis=0)
    total = sum(seg_lens)
    m = l = acc = None
    for c0 in range(0, total, chunk):
        c1 = min(c0 + chunk, total)
        s = lax.dot_general(q, k_scr[c0:c1, :], (((1,), (1,)), ((), ())), preferred_element_type=F32)
        mc = jnp.max(s, axis=-1, keepdims=True)
        if m is None:
            m_new = mc
        else:
            m_new = jnp.maximum(m, mc)
            alpha = jnp.exp2(m - m_new)
        p = jnp.exp2(s - m_new)
        pv = jnp.dot(p.astype(BF), v_scr[c0:c1, :], preferred_element_type=F32)
        ps = jnp.sum(p, axis=-1, keepdims=True)
        if m is None:
            l, acc = ps, pv
        else:
            l, acc = alpha * l + ps, alpha * acc + pv
        m = m_new
    o = acc / l
    for g in range(group):
        o_ref[:, g * dv:(g + 1) * dv] = o[g * tq:(g + 1) * tq].astype(o_ref.dtype)


def _attention(name, q, segs, *, q_rows, n_kv, group, dk, dv, tq):
    nq = q_rows // tq
    n_parts = len(segs[0][1])
    in_specs = [pl.BlockSpec((tq, group * dk), lambda b, h, i: (b * nq + i, h))]
    operands = [q]
    for length, k_parts, _ in segs:
        for arr, col in k_parts:
            in_specs.append(pl.BlockSpec((length, LANES), lambda b, h, i, col=col: (b, col(h))))
            operands.append(arr)
    for length, _, (arr, col) in segs:
        in_specs.append(pl.BlockSpec((length, dv), lambda b, h, i, col=col: (b, col(h))))
        operands.append(arr)
    total = sum(s[0] for s in segs)
    kern = functools.partial(_attn_kernel, n_seg=len(segs), n_parts=n_parts, group=group, dk=dk, dv=dv,
                             tq=tq, seg_lens=tuple(s[0] for s in segs), chunk=ATTN_KEY_CHUNK)
    return pl.pallas_call(
        kern,
        grid=(BATCH, n_kv, nq),
        in_specs=in_specs,
        out_specs=pl.BlockSpec((tq, group * dv), lambda b, h, i: (b * nq + i, h)),
        out_shape=jax.ShapeDtypeStruct((BATCH * q_rows, n_kv * group * dv), BF),
        scratch_shapes=[pltpu.VMEM((total, dk), BF), pltpu.VMEM((total, dv), BF)],
        compiler_params=_params(3),
        name=name,
    )(*operands)


def _na_band_start(g):
    return jnp.clip(NA_GROUP_ROWS * g - NA_WIN_H // 2, 0, GRID_H - NA_BAND_ROWS)


def _na_kernel(q_ref, k_ref, v_ref, kc_ref, vc_ref, bias_ref, o_ref):
    start = pl.multiple_of(_na_band_start(pl.program_id(2)) * GRID_W, GRID_W)
    band = NA_BAND_ROWS * GRID_W
    kb = k_ref[pl.ds(start, band), :]
    vb = v_ref[pl.ds(start, band), :]
    q = q_ref[...]
    nt = (((1,), (1,)), ((), ()))
    s_b = lax.dot_general(q, kb, nt, preferred_element_type=F32) + bias_ref[...]
    s_c = lax.dot_general(q, kc_ref[...], nt, preferred_element_type=F32)
    m = jnp.maximum(jnp.max(s_b, axis=-1, keepdims=True), jnp.max(s_c, axis=-1, keepdims=True))
    p_b = jnp.exp2(s_b - m)
    p_c = jnp.exp2(s_c - m)
    l = jnp.sum(p_b, axis=-1, keepdims=True) + jnp.sum(p_c, axis=-1, keepdims=True)
    o = (jnp.dot(p_b.astype(BF), vb, preferred_element_type=F32)
         + jnp.dot(p_c.astype(BF), vc_ref[...], preferred_element_type=F32))
    o_ref[...] = (o / l).astype(o_ref.dtype)


def _na_bias_tables(rpb):
    ri = np.arange(NA_GROUP_ROWS)
    bi = np.arange(NA_BAND_ROWS)
    col = np.arange(GRID_W)
    c0 = np.clip(col - NA_WIN_W // 2, 0, GRID_W - NA_WIN_W)
    col_ok = (col[None, :] >= c0[:, None]) & (col[None, :] < c0[:, None] + NA_WIN_W)
    row_idx, ok = [], []
    for g in (0, 1, NA_N_GROUPS - 1):
        r = NA_GROUP_ROWS * g + ri
        r0 = np.clip(r - NA_WIN_H // 2, 0, GRID_H - NA_WIN_H)
        start = min(int(r0[0]), GRID_H - NA_BAND_ROWS)
        kr = start + bi
        row_ok = (kr[None, :] >= r0[:, None]) & (kr[None, :] < r0[:, None] + NA_WIN_H)
        row_idx.append(np.clip(kr[None, :] - r[:, None] + NA_WIN_H - 1, 0, 2 * NA_WIN_H - 2))
        ok.append(row_ok[:, None, :, None] & col_ok[None, :, None, :])
    n_rows = 3 * NA_GROUP_ROWS * NA_BAND_ROWS
    by_row = jnp.take(rpb, np.stack(row_idx).reshape(-1), axis=1)
    pad = GRID_W - NA_WIN_W
    f = jnp.pad(by_row, ((0, 0), (0, 0), (pad, pad)))
    flat = jnp.tile(f, (1, 1, GRID_W))[:, :, GRID_W - 1:GRID_W - 1 + GRID_W * (2 * GRID_W - 2)]
    t = flat.reshape(NA_HEADS, n_rows, GRID_W, 2 * GRID_W - 2)[..., :GRID_W]
    t = t.reshape(NA_HEADS, 3, NA_GROUP_ROWS, NA_BAND_ROWS, GRID_W, GRID_W).transpose(0, 1, 2, 4, 3, 5)
    t = jnp.where(np.stack(ok)[None], t * LOG2E, -jnp.inf)
    return t.reshape(NA_HEADS, 3, NA_GROUP_ROWS * GRID_W, NA_BAND_ROWS * GRID_W)


def _na_attention(q, k, v, kc, vc, bias):
    tq = NA_GROUP_ROWS * GRID_W
    band = NA_BAND_ROWS * GRID_W

    def variant(g):
        return jnp.minimum(g, 1) + (g == NA_N_GROUPS - 1).astype(jnp.int32)

    head_block = lambda b, h, g: (b, h)
    return pl.pallas_call(
        _na_kernel,
        grid=(BATCH, NA_HEADS, NA_N_GROUPS),
        in_specs=[
            pl.BlockSpec((tq, HEAD_DIM), lambda b, h, g: (b * NA_N_GROUPS + g, h)),
            pl.BlockSpec((SEQ, HEAD_DIM), head_block),
            pl.BlockSpec((SEQ, HEAD_DIM), head_block),
            pl.BlockSpec((CTX_LEN, HEAD_DIM), head_block),
            pl.BlockSpec((CTX_LEN, HEAD_DIM), head_block),
            pl.BlockSpec((None, None, tq, band), lambda b, h, g: (h, variant(g), 0, 0)),
        ],
        out_specs=pl.BlockSpec((tq, HEAD_DIM), lambda b, h, g: (b * NA_N_GROUPS + g, h)),
        out_shape=jax.ShapeDtypeStruct((BATCH * SEQ, NA_WIDTH), BF),
        compiler_params=_params(3),
        name="na_attention",
    )(q, k, v, kc, vc, bias)


def _head_gain(g, width):
    return jnp.tile(g, width // g.shape[0]).reshape(1, width)


def _ffn_up_kernel(*refs, cast_w2):
    if cast_w2:
        x_ref, wg_ref, wu_ref, w2_ref, o_ref, w2_bf_ref = refs
        w2_bf_ref[...] = w2_ref[...].astype(BF)
    else:
        x_ref, wg_ref, wu_ref, o_ref = refs
    x = x_ref[...]
    gate = jnp.dot(x, wg_ref[...].astype(BF), preferred_element_type=F32)
    up = jnp.dot(x, wu_ref[...].astype(BF), preferred_element_type=F32)
    o_ref[...] = (gate * jax.nn.sigmoid(gate) * up).astype(o_ref.dtype)


def _ffn_up(hn, w1, w2, layer, f, tok, cast_w2):
    tm = 2048 if tok.latent else tok.rows
    tn = 256
    ni, nj = tok.rows // tm, D_FF // tn
    in_specs = [
        pl.BlockSpec((tm, D_MODEL), lambda i, j: (i, 0), pipeline_mode=pl.Buffered(1)),
        pl.BlockSpec((None, None, D_MODEL, tn), lambda i, j: (layer, f, 0, j)),
        pl.BlockSpec((None, None, D_MODEL, tn), lambda i, j: (layer, f, 0, nj + j)),
    ]
    out_specs = [pl.BlockSpec((tm, tn), lambda i, j: (i, j))]
    out_shape = [jax.ShapeDtypeStruct((tok.rows, D_FF), BF)]
    operands = [hn, w1, w1]
    if cast_w2:
        assert D_FF % (ni * nj) == 0
        slab = D_FF // (ni * nj)
        in_specs.append(pl.BlockSpec((None, None, slab, D_MODEL), lambda i, j: (layer, f, i * nj + j, 0)))
        out_specs.append(pl.BlockSpec((slab, D_MODEL), lambda i, j: (i * nj + j, 0)))
        out_shape.append(jax.ShapeDtypeStruct((D_FF, D_MODEL), BF))
        operands.append(w2)
    return pl.pallas_call(
        functools.partial(_ffn_up_kernel, cast_w2=cast_w2),
        grid=(ni, nj),
        in_specs=in_specs,
        out_specs=out_specs,
        out_shape=out_shape,
        compiler_params=_params(2),
        name="ffn_up",
    )(*operands)


def _ffn(h, g, mt, layer, f, w1, w2, tok, w2_bf=None):
    chunk = 6 * f
    hn = _normmod(h, g, mt, layer, chunk, tok)
    if w2_bf is None:
        u, w2_bf = _ffn_up(hn, w1, w2, layer, f, tok, True)
    else:
        u, = _ffn_up(hn, w1, w2, layer, f, tok, False)
    tn = 256
    out, = _matmul(
        "ffn_down", u, [(w2_bf, (), 0)],
        [(h, pl.BlockSpec((tok.tm, tn), lambda i, j: (i, j))), (mt, _mod_spec(tok, layer, chunk + 2, tn))],
        [((tok.rows, D_MODEL), F32, pl.BlockSpec((tok.tm, tn), lambda i, j: (i, j)))],
        _epi_residual(0.5), rows=tok.rows, tm=tok.tm, tn=tn, tk=D_FF, n_cols=D_MODEL, lhs_single=True)
    return out, w2_bf


def _out_proj(name, o, w, prefix, h, mt, layer, tok):
    tn = 512
    out, = _matmul(
        name, o, [(w, prefix, 0)],
        [(h, pl.BlockSpec((tok.tm, tn), lambda i, j: (i, j))), (mt, _mod_spec(tok, layer, 5, tn))],
        [((tok.rows, D_MODEL), F32, pl.BlockSpec((tok.tm, tn), lambda i, j: (i, j)))],
        _epi_residual(1.0), rows=tok.rows, tm=tok.tm, tn=tn, tk=o.shape[1], n_cols=D_MODEL)
    return out


def _head_proj(name, u, w, prefix, col_off, width, kinds, gain, scale, tabs, shifts, tok, tn=512):
    out, = _matmul(
        name, u, [(w, prefix, col_off // tn)],
        [(gain, pl.BlockSpec((1, tn), lambda i, j: (0, j)))] + _rope_extras(tabs, tok),
        [((tok.rows, width), BF, pl.BlockSpec((tok.tm, tn), lambda i, j: (i, j)))],
        _epi_heads(kinds * (tn // (LANES * len(kinds))), scale, shifts),
        rows=tok.rows, tm=tok.tm, tn=tn, tk=u.shape[1], n_cols=width)
    return out


def _even_projections(u, w_in, j, gains, tok, want_q):
    na_q_g, na_k_g, gq_q_g, gq_k_g = gains
    scale = HEAD_DIM ** -0.5 * LOG2E
    rot ="NR" if tok.latent else "N"
    tabs = _rope128_tables() if tok.latent else []
    ones = jnp.ones((HEAD_DIM,), F32)
    proj = functools.partial(_head_proj, u=u, w=w_in, prefix=(j,), tok=tok)
    out = {}
    if want_q:
        out["na_q"] = proj("na_q", col_off=0, width=NA_WIDTH, kinds=("N",),
                           gain=_head_gain(na_q_g, NA_WIDTH), scale=scale, tabs=[], shifts=())
        out["gq_q"] = proj("gq_q", col_off=NA_WIDTH, width=GQA_Q_WIDTH, kinds=(rot,),
                           gain=_head_gain(gq_q_g, GQA_Q_WIDTH), scale=scale, tabs=tabs, shifts=_ROPE128_SHIFTS)
    out["na_k"] = proj("na_k", col_off=EV_Q_COLS, width=NA_WIDTH, kinds=("N",),
                       gain=_head_gain(na_k_g, NA_WIDTH), scale=1.0, tabs=[], shifts=())
    out["na_v"] = proj("na_v", col_off=EV_Q_COLS + NA_WIDTH, width=NA_WIDTH, kinds=("P",),
                       gain=_head_gain(ones, NA_WIDTH), scale=1.0, tabs=[], shifts=())
    out["gq_k"] = proj("gq_k", col_off=EV_Q_COLS + 2 * NA_WIDTH, width=GQA_KV_WIDTH, kinds=(rot,),
                       gain=_head_gain(gq_k_g, GQA_KV_WIDTH), scale=1.0, tabs=tabs, shifts=_ROPE128_SHIFTS)
    out["gq_v"] = proj("gq_v", col_off=EV_Q_COLS + 2 * NA_WIDTH + GQA_KV_WIDTH, width=GQA_KV_WIDTH,
                       kinds=("P",), gain=_head_gain(ones, GQA_KV_WIDTH), scale=1.0, tabs=[], shifts=())
    return out


def _even_mixer(ul, uc, hl, hc, mt, layer, w_in, w_out, j, gains, rpb, lat, ctx, ctx_out):
    pl_ = _even_projections(ul, w_in, j, gains, lat, True)
    pc = _even_projections(uc, w_in, j, gains, ctx, ctx_out)
    head = lambda h: h
    a = _na_attention(pl_["na_q"], pl_["na_k"], pl_["na_v"], pc["na_k"], pc["na_v"], _na_bias_tables(rpb))
    gq_segs = [(CTX_LEN, [(pc["gq_k"], head)], (pc["gq_v"], head)),
               (SEQ, [(pl_["gq_k"], head)], (pl_["gq_v"], head))]
    b = _attention("gqa", pl_["gq_q"], gq_segs, q_rows=SEQ, n_kv=GQA_KV_HEADS, group=GQA_GROUP,
                   dk=HEAD_DIM, dv=HEAD_DIM, tq=128)
    hl = _out_proj("ev_out", jnp.concatenate([a, b], axis=-1), w_out, (j,), hl, mt, layer, lat)
    if not ctx_out:
        return hl, hc
    ac = _attention("na_ctx", pc["na_q"], [(CTX_LEN, [(pc["na_k"], head)], (pc["na_v"], head))],
                    q_rows=CTX_LEN, n_kv=NA_HEADS, group=1, dk=HEAD_DIM, dv=HEAD_DIM, tq=CTX_LEN)
    bc = _attention("gqa_ctx", pc["gq_q"], gq_segs[:1], q_rows=CTX_LEN, n_kv=GQA_KV_HEADS, group=GQA_GROUP,
                    dk=HEAD_DIM, dv=HEAD_DIM, tq=128)
    hc = _out_proj("ev_out_ctx", jnp.concatenate([ac, bc], axis=-1), w_out, (j,), hc, mt, layer, ctx)
    return hl, hc


def _mla_down(u, w_down_p, gains, tok):
    q_a_g, kv_a_g, kr_g = gains
    tabs = _rope64_tables() if tok.latent else []
    tm = 512
    t = _Tokens(tok.rows, tm, tok.latent)
    full = lambda n: pl.BlockSpec((1, n), lambda i, j: (0, 0))
    rows = lambda n: pl.BlockSpec((tm, n), lambda i, j: (i, 0))
    return _matmul(
        "mla_down", u, [(w_down_p, (), 0)],
        [(q_a_g.reshape(1, -1), full(MLA_Q_RANK)), (kv_a_g.reshape(1, -1), full(MLA_KV_RANK)),
         (kr_g, full(LANES))] + _rope_extras(tabs, t),
        [((tok.rows, MLA_Q_RANK), BF, rows(MLA_Q_RANK)), ((tok.rows, MLA_KV_RANK), BF, rows(MLA_KV_RANK)),
         ((tok.rows, LANES), BF, rows(LANES))],
        _epi_mla_down(tok.latent), rows=tok.rows, tm=tm, tn=MLA_DOWN_COLS, tk=D_MODEL, n_cols=MLA_DOWN_COLS)


def _mla_mixer(ul, uc, hl, hc, mt, layer, w, g, lat, ctx, ctx_out):
    pad64 = lambda v: jnp.concatenate([v, jnp.zeros((LANES - MLA_ROPE,), F32)])
    kr_gain = pad64(g["kr"]).reshape(1, LANES)
    q_gain = _head_gain(jnp.concatenate([g["qn"], pad64(g["qr"])]), MLA_HEADS * MLA_Q_SLOT)
    kv_gain = _head_gain(jnp.concatenate([g["kn"], jnp.ones((MLA_V,), F32)]), MLA_HEADS * (MLA_NOPE + MLA_V))
    down_gains = (g["q_a"], g["kv_a"], kr_gain)
    tabs = _rope64_tables()

    def keys_values(ckv, tok):
        return _head_proj("mla_kv", ckv, w["ukv"], (), 0, MLA_HEADS * (MLA_NOPE + MLA_V), ("N", "P"),
                          kv_gain, 1.0, [], (), tok)

    cq_l, ckv_l, kr_l = _mla_down(ul, w["down"], down_gains, lat)
    cq_c, ckv_c, kr_c = _mla_down(uc, w["down"], down_gains, ctx)
    kv_l = keys_values(ckv_l, lat)
    kv_c = keys_values(ckv_c, ctx)
    scale = MLA_QK ** -0.5 * LOG2E
    q_l = _head_proj("mla_q", cq_l, w["uq"], (), 0, MLA_HEADS * MLA_Q_SLOT, ("N", "MR"), q_gain, scale,
                     tabs, _ROPE64_SHIFTS, lat)
    nope = lambda h: 2 * h
    val = lambda h: 2 * h + 1
    shared = lambda h: 0
    segs = [(CTX_LEN, [(kv_c, nope), (kr_c, shared)], (kv_c, val)),
            (SEQ, [(kv_l, nope), (kr_l, shared)], (kv_l, val))]
    o = _attention("mla", q_l, segs, q_rows=SEQ, n_kv=MLA_HEADS, group=1, dk=MLA_Q_SLOT, dv=MLA_V, tq=512)
    hl = _out_proj("mla_out", o, w["o"], (), hl, mt, layer, lat)
    if not ctx_out:
        return hl, hc
    q_c = _head_proj("mla_q_ctx", cq_c, w["uq"], (), 0, MLA_HEADS * MLA_Q_SLOT, ("N", "M"), q_gain, scale,
                     [], (), ctx)
    oc = _attention("mla_ctx", q_c, segs[:1], q_rows=CTX_LEN, n_kv=MLA_HEADS, group=1, dk=MLA_Q_SLOT,
                    dv=MLA_V, tq=CTX_LEN)
    hc = _out_proj("mla_out_ctx", oc, w["o"], (), hc, mt, layer, ctx)
    return hl, hc


def _mla_weights(w_down, w_uq, w_ukv, w_o, j):
    down = jnp.pad(w_down[j].astype(BF), ((0, 0), (0, LANES - MLA_ROPE)))
    uq = w_uq[j].astype(BF).reshape(MLA_Q_RANK, MLA_HEADS, MLA_QK)
    uq = jnp.pad(uq, ((0, 0), (0, 0), (0, MLA_Q_SLOT - MLA_QK))).reshape(MLA_Q_RANK, MLA_HEADS * MLA_Q_SLOT)
    return {"down": down, "uq": uq, "ukv": w_ukv[j].astype(BF), "o": w_o[j].astype(BF)}


def kernel(x, c, ctx, c_ctx, norm_g, w_mod, b_mod, ffn_w1, ffn_w2, ev_w_in, ev_w_out, na_q_g, na_k_g, na_rpb, gq_q_g, gq_k_g, mla_w_down, mla_q_a_g, mla_kv_a_g, mla_w_uq, mla_w_ukv, mla_qn_g, mla_qr_g, mla_kn_g, mla_kr_g, mla_w_o):
    lat = _Tokens(BATCH * SEQ, 1024, True)
    ctk = _Tokens(BATCH * CTX_LEN, 512, False)
    cvec = jnp.zeros((MOD_ROWS, D_MODEL), F32).at[:BATCH].set(c).at[CTX_MOD_ROW].set(c_ctx)
    mt = _modulation(cvec, w_mod, b_mod).reshape(DEPTH, MOD_ROWS, 1, N_MOD * D_MODEL)
    w_in = ev_w_in.astype(BF)
    w_out = ev_w_out.astype(BF)
    hl = x.reshape(BATCH * SEQ, D_MODEL)
    hc = ctx.reshape(BATCH * CTX_LEN, D_MODEL)
    for i in range(DEPTH):
        last = i == DEPTH - 1
        g = norm_g[i]
        hl, w2_bf = _ffn(hl, g[0], mt, i, 0, ffn_w1, ffn_w2, lat)
        hc, _ = _ffn(hc, g[0], mt, i, 0, ffn_w1, ffn_w2, ctk, w2_bf)
        ul = _normmod(hl, g[1], mt, i, 3, lat)
        uc = _normmod(hc, g[1], mt, i, 3, ctk)
        j = i // 2
        if i % 2 == 0:
            gains = (na_q_g[j], na_k_g[j], gq_q_g[j], gq_k_g[j])
            hl, hc = _even_mixer(ul, uc, hl, hc, mt, i, w_in, w_out, j, gains, na_rpb[j], lat, ctk, not last)
        else:
            w = _mla_weights(mla_w_down, mla_w_uq, mla_w_ukv, mla_w_o, j)
            gains = {"q_a": mla_q_a_g[j], "kv_a": mla_kv_a_g[j], "qn": mla_qn_g[j], "qr": mla_qr_g[j],
                     "kn": mla_kn_g[j], "kr": mla_kr_g[j]}
            hl, hc = _mla_mixer(ul, uc, hl, hc, mt, i, w, gains, lat, ctk, not last)
        hl, w2_bf = _ffn(hl, g[2], mt, i, 1, ffn_w1, ffn_w2, lat)
        if not last:
            hc, _ = _ffn(hc, g[2], mt, i, 1, ffn_w1, ffn_w2, ctk, w2_bf)
    return hl.reshape(BATCH, SEQ, D_MODEL)
```

```python
import functools

import numpy as np
import jax
import jax.numpy as jnp
from jax import lax
from jax.experimental import pallas as pl
from jax.experimental.pallas import tpu as pltpu

D_MODEL = 4096
BATCH = 2
SEQ = 4096
DEPTH = 2
CTX_LEN = 256
GRID_W = 64
GRID_H = SEQ // GRID_W
HEAD_DIM = 128
EPS = 1e-6
ROPE_THETA = 10000.0
N_MOD = 9
D_FF = 10240
NA_HEADS = 16
NA_WIN_H = 8
NA_WIN_W = 16
GQA_Q_HEADS = 16
GQA_KV_HEADS = 4
GQA_GROUP = GQA_Q_HEADS // GQA_KV_HEADS
NA_WIDTH = NA_HEADS * HEAD_DIM
GQA_Q_WIDTH = GQA_Q_HEADS * HEAD_DIM
GQA_KV_WIDTH = GQA_KV_HEADS * HEAD_DIM
EV_Q_COLS = NA_WIDTH + GQA_Q_WIDTH
MLA_HEADS = 32
MLA_Q_RANK = 1024
MLA_KV_RANK = 512
MLA_NOPE = 128
MLA_ROPE = 64
MLA_V = 128
MLA_QK = MLA_NOPE + MLA_ROPE

LANES = 128
MLA_Q_SLOT = 2 * LANES
MLA_DOWN_COLS = MLA_Q_RANK + MLA_KV_RANK + LANES
MOD_ROWS = 8
CTX_MOD_ROW = BATCH

NA_GROUP_ROWS = 4
NA_BAND_ROWS = 12
NA_N_GROUPS = GRID_H // NA_GROUP_ROWS

ATTN_KEY_CHUNK = 256
LOG2E = 1.4426950408889634

VMEM_LIMIT = 56 * 1024 * 1024

BF = jnp.bfloat16
F32 = jnp.float32


def _params(n_axes):
    return pltpu.CompilerParams(dimension_semantics=("arbitrary",) * n_axes, vmem_limit_bytes=VMEM_LIMIT)


def _mod_kernel(c_ref, w_ref, b_ref, o_ref):
    c = c_ref[...]
    a = (c * jax.nn.sigmoid(c)).astype(BF)
    o_ref[...] = jnp.dot(a, w_ref[...].astype(BF), preferred_element_type=F32) + b_ref[...]


def _modulation(cvec, w_mod, b_mod):
    n = N_MOD * D_MODEL
    tn = 512
    return pl.pallas_call(
        _mod_kernel,
        grid=(DEPTH, n // tn),
        in_specs=[
            pl.BlockSpec((MOD_ROWS, D_MODEL), lambda l, j: (0, 0)),
            pl.BlockSpec((None, D_MODEL, tn), lambda l, j: (l, 0, j)),
            pl.BlockSpec((None, 1, tn), lambda l, j: (l, 0, j)),
        ],
        out_specs=pl.BlockSpec((None, MOD_ROWS, tn), lambda l, j: (l, 0, j)),
        out_shape=jax.ShapeDtypeStruct((DEPTH, MOD_ROWS, n), F32),
        compiler_params=_params(2),
        name="modulation",
    )(cvec, w_mod, b_mod.reshape(DEPTH, 1, n))


class _Tokens:
    def __init__(self, rows, tm, latent):
        self.rows, self.tm, self.latent = rows, tm, latent

    def mod_row(self, i):
        return i // (SEQ // self.tm) if self.latent else CTX_MOD_ROW

    def pos_block(self, i):
        return i % (SEQ // self.tm)


def _mod_spec(tok, layer, chunk, tn):
    return pl.BlockSpec((None, None, 1, tn),
                        lambda i, j: (layer, tok.mod_row(i), 0, chunk * (D_MODEL // tn) + j))


def _normmod_kernel(x_ref, g_ref, sh_ref, sc_ref, o_ref):
    x = x_ref[...]
    ms = jnp.mean(x * x, axis=-1, keepdims=True)
    y = x * lax.rsqrt(ms + EPS) * g_ref[...]
    o_ref[...] = (y * (1.0 + sc_ref[...]) + sh_ref[...]).astype(o_ref.dtype)


def _normmod(x, g, mt, layer, chunk, tok):
    tm = 256
    t = _Tokens(tok.rows, tm, tok.latent)
    sh = _mod_spec(t, layer, chunk, D_MODEL)
    sc = _mod_spec(t, layer, chunk + 1, D_MODEL)
    return pl.pallas_call(
        _normmod_kernel,
        grid=(tok.rows // tm,),
        in_specs=[
            pl.BlockSpec((tm, D_MODEL), lambda i: (i, 0)),
            pl.BlockSpec((1, D_MODEL), lambda i: (0, 0)),
            pl.BlockSpec(sh.block_shape, lambda i: sh.index_map(i, 0)),
            pl.BlockSpec(sc.block_shape, lambda i: sc.index_map(i, 0)),
        ],
        out_specs=pl.BlockSpec((tm, D_MODEL), lambda i: (i, 0)),
        out_shape=jax.ShapeDtypeStruct((tok.rows, D_MODEL), BF),
        compiler_params=_params(1),
        name="normmod",
    )(x, g.reshape(1, D_MODEL), mt, mt)


def _mm_kernel(*refs, n_rhs, n_extra, n_out, groups, epilogue):
    lhs_ref = refs[0]
    rhs_refs = refs[1:1 + n_rhs]
    extra = refs[1 + n_rhs:1 + n_rhs + n_extra]
    out_refs = refs[1 + n_rhs + n_extra:]
    acc_refs = out_refs[n_out:]
    out_refs = out_refs[:n_out]
    a = lhs_ref[...]
    if not acc_refs:
        for c0, c1 in groups:
            accs = [jnp.dot(a, r[:, c0:c1].astype(BF), preferred_element_type=F32) for r in rhs_refs]
            epilogue(c0, c1, accs, extra, out_refs)
        return

    @pl.when(pl.program_id(0) == 0)
    def _():
        for acc in acc_refs:
            acc[...] = jnp.zeros_like(acc)

    for c0, c1 in groups:
        epilogue(c0, c1, [acc[:, c0:c1] for acc in acc_refs], extra, out_refs)
    for c0, c1 in groups:
        for r, acc in zip(rhs_refs, acc_refs):
            acc[:, c0:c1] = jnp.dot(a, r[:, c0:c1].astype(BF), preferred_element_type=F32)


MXU_COLS = 256


def _matmul(name, lhs, rhs, extras, outs, epilogue, *, rows, tm, tn, n_cols, lhs_single=False, groups=None,
            lagged=False):
    if groups is None:
        groups = tuple((c, c + MXU_COLS) for c in range(0, tn, MXU_COLS))
    kdim = lhs.shape[1]
    ni, nj = rows // tm, n_cols // tn
    if lagged:
        n_tiles = ni * nj
        grid = (n_tiles + 1,)
        cur = lambda f: (lambda t: f(*divmod(jnp.minimum(t, n_tiles - 1), nj)))
        prev = lambda f: (lambda t: f(*divmod(jnp.maximum(t - 1, 0), nj)))
        scratch = [pltpu.VMEM((tm, tn), F32) for _ in rhs]
    else:
        grid = (ni, nj)
        cur = prev = lambda f: f
        scratch = []
    lhs_mode = pl.Buffered(1) if lhs_single else None
    in_specs = [pl.BlockSpec((tm, kdim), cur(lambda i, j: (i, 0)), pipeline_mode=lhs_mode)]
    for _, prefix, off in rhs:
        in_specs.append(pl.BlockSpec((None,) * len(prefix) + (kdim, tn),
                                     cur(lambda i, j, prefix=prefix, off=off: prefix + (0, off + j))))
    in_specs += [pl.BlockSpec(spec.block_shape, prev(spec.index_map)) for _, spec in extras]
    kern = functools.partial(_mm_kernel, n_rhs=len(rhs), n_extra=len(extras), n_out=len(outs),
                             groups=groups, epilogue=epilogue)
    return pl.pallas_call(
        kern,
        grid=grid,
        in_specs=in_specs,
        out_specs=[pl.BlockSpec(spec.block_shape, prev(spec.index_map)) for _, _, spec in outs],
        out_shape=[jax.ShapeDtypeStruct(shape, dtype) for shape, dtype, _ in outs],
        scratch_shapes=scratch,
        compiler_params=_params(len(grid)),
        name=name,
    )(lhs, *[r[0] for r in rhs], *[e[0] for e in extras])


def _epi_residual(coef):
    def epi(c0, c1, accs, extra, outs):
        res_ref, gate_ref = extra
        outs[0][:, c0:c1] = res_ref[:, c0:c1] + (coef * gate_ref[:, c0:c1]) * accs[0]
    return epi


def _rms_chunk(x, gain):
    ms = jnp.mean(x * x, axis=-1, keepdims=True)
    return x * lax.rsqrt(ms + EPS) * gain


def _rotate(x, tabs, shift):
    return x * tabs[0][...] + pltpu.roll(x, shift, 1) * tabs[1][...]


def _epi_heads(kinds, shift):
    def epi(c0, c1, accs, extra, outs):
        gain_ref, tabs = extra[0], extra[1:]
        for c in range(c0, c1, LANES):
            kind = kinds[c // LANES]
            x = accs[0][:, c - c0:c - c0 + LANES]
            if kind != "P":
                x = _rms_chunk(x, gain_ref[:, c:c + LANES])
            if kind == "NR":
                x = _rotate(x, tabs, shift)
            outs[0][:, c:c + LANES] = x.astype(outs[0].dtype)
    return epi


_MLA_DOWN_GROUPS = ((0, MLA_Q_RANK), (MLA_Q_RANK, MLA_Q_RANK + MLA_KV_RANK),
                    (MLA_Q_RANK + MLA_KV_RANK, MLA_DOWN_COLS))


def _epi_mla_down(rope):
    def epi(c0, c1, accs, extra, outs):
        which = [g[0] for g in _MLA_DOWN_GROUPS].index(c0)
        x = _rms_chunk(accs[0], extra[which][...])
        if rope and which == 2:
            x = _rotate(x, extra[3:], _ROPE64_SHIFT)
        outs[which][...] = x.astype(outs[which].dtype)
    return epi


def _axial_angles(rot_dim):
    t = jnp.arange(SEQ)
    row = (t // GRID_W).astype(F32)
    col = (t % GRID_W).astype(F32)
    axis_dim = rot_dim // 2
    inv_freq = ROPE_THETA ** (-jnp.arange(0, axis_dim, 2, dtype=F32) / axis_dim)
    ang = jnp.concatenate([row[:, None] * inv_freq, col[:, None] * inv_freq], axis=-1)
    return jnp.cos(ang), jnp.sin(ang)


_ROPE128_SHIFT = HEAD_DIM // 2
_ROPE64_SHIFT = LANES - MLA_ROPE // 2


def _rope128_tables():
    cos, sin = _axial_angles(HEAD_DIM)
    return [jnp.concatenate([cos, cos], -1), jnp.concatenate([-sin, sin], -1)]


def _rope64_tables():
    cos, sin = _axial_angles(MLA_ROPE)
    z64 = jnp.zeros((SEQ, LANES - MLA_ROPE), F32)
    return [jnp.concatenate([cos, cos, z64], -1), jnp.concatenate([-sin, sin, z64], -1)]


def _rope_extras(tabs, tok):
    return [(t, pl.BlockSpec((tok.tm, LANES), lambda i, j: (tok.pos_block(i), 0))) for t in tabs]


def _attn_kernel(*refs, n_seg, n_parts, group, dk, dv, tq, seg_lens, chunk):
    q_ref = refs[0]
    k_refs = refs[1:1 + n_seg * n_parts]
    v_refs = refs[1 + n_seg * n_parts:1 + n_seg * n_parts + n_seg]
    o_ref, k_scr, v_scr = refs[1 + n_seg * n_parts + n_seg:]

    @pl.when(pl.program_id(2) == 0)
    def _():
        off = 0
        for s in range(n_seg):
            length = seg_lens[s]
            col = 0
            for part in k_refs[s * n_parts:(s + 1) * n_parts]:
                width = part.shape[-1]
                k_scr[off:off + length, col:col + width] = part[...]
                col += width
            v_scr[off:off + length, :dv] = v_refs[s][...]
            off += length
        lane = lax.broadcasted_iota(jnp.int32, (off, LANES), 1)
        v_scr[:, dv:] = jnp.where(lane == 0, 1.0, 0.0).astype(v_scr.dtype)

    q = q_ref[...]
    if group > 1:
        q = jnp.concatenate([q[:, g * dk:(g + 1) * dk] for g in range(group)], axis=0)
    total = sum(seg_lens)
    m = acc = None
    for c0 in range(0, total, chunk):
        c1 = min(c0 + chunk, total)
        s = lax.dot_general(q, k_scr[c0:c1, :], (((1,), (1,)), ((), ())), preferred_element_type=F32)
        mc = jnp.max(s, axis=-1, keepdims=True)
        m_new = mc if m is None else jnp.maximum(m, mc)
        p = jnp.exp2((s - m_new).astype(BF))
        pv = jnp.dot(p, v_scr[c0:c1, :], preferred_element_type=F32)
        acc = pv if m is None else jnp.exp2(m - m_new) * acc + pv
        m = m_new
    o = acc[:, :dv] / acc[:, dv:dv + 1]
    for g in range(group):
        o_ref[:, g * dv:(g + 1) * dv] = o[g * tq:(g + 1) * tq].astype(o_ref.dtype)


def _attention(name, q, segs, *, q_rows, n_kv, group, dk, dv, tq):
    nq = q_rows // tq
    n_parts = len(segs[0][1])
    in_specs = [pl.BlockSpec((tq, group * dk), lambda b, h, i: (b * nq + i, h))]
    operands = [q]
    for length, k_parts, _ in segs:
        for arr, col in k_parts:
            in_specs.append(pl.BlockSpec((length, LANES), lambda b, h, i, col=col: (b, col(h))))
            operands.append(arr)
    for length, _, (arr, col) in segs:
        in_specs.append(pl.BlockSpec((length, dv), lambda b, h, i, col=col: (b, col(h))))
        operands.append(arr)
    total = sum(s[0] for s in segs)
    kern = functools.partial(_attn_kernel, n_seg=len(segs), n_parts=n_parts, group=group, dk=dk, dv=dv,
                             tq=tq, seg_lens=tuple(s[0] for s in segs), chunk=ATTN_KEY_CHUNK)
    return pl.pallas_call(
        kern,
        grid=(BATCH, n_kv, nq),
        in_specs=in_specs,
        out_specs=pl.BlockSpec((tq, group * dv), lambda b, h, i: (b * nq + i, h)),
        out_shape=jax.ShapeDtypeStruct((BATCH * q_rows, n_kv * group * dv), BF),
        scratch_shapes=[pltpu.VMEM((total, dk), BF), pltpu.VMEM((total, dv + LANES), BF)],
        compiler_params=_params(3),
        name=name,
    )(*operands)


def _na_band_start(g):
    return min(max(NA_GROUP_ROWS * g - NA_WIN_H // 2, 0), GRID_H - NA_BAND_ROWS)


def _na_variant(g):
    return min(g, 1) + (g == NA_N_GROUPS - 1)


def _na_kernel(q_ref, k_ref, v_ref, kc_ref, vc_ref, bias_ref, o_ref):
    tq = NA_GROUP_ROWS * GRID_W
    band = NA_BAND_ROWS * GRID_W
    nt = (((1,), (1,)), ((), ()))
    kc = kc_ref[...]
    vc = vc_ref[...]
    for g in range(NA_N_GROUPS):
        start = _na_band_start(g) * GRID_W
        q = q_ref[g * tq:(g + 1) * tq, :]
        kb = k_ref[start:start + band, :]
        vb = v_ref[start:start + band, :]
        s_b = lax.dot_general(q, kb, nt, preferred_element_type=F32) + bias_ref[_na_variant(g)]
        s_c = lax.dot_general(q, kc, nt, preferred_element_type=F32)
        m = jnp.maximum(jnp.max(s_b, axis=-1, keepdims=True), jnp.max(s_c, axis=-1, keepdims=True))
        p_b = jnp.exp2(s_b - m)
        p_c = jnp.exp2(s_c - m)
        l = jnp.sum(p_b, axis=-1, keepdims=True) + jnp.sum(p_c, axis=-1, keepdims=True)
        o = (jnp.dot(p_b.astype(BF), vb, preferred_element_type=F32)
             + jnp.dot(p_c.astype(BF), vc, preferred_element_type=F32))
        o_ref[g * tq:(g + 1) * tq, :] = (o / l).astype(o_ref.dtype)


def _na_bias_tables(rpb):
    ri = np.arange(NA_GROUP_ROWS)
    bi = np.arange(NA_BAND_ROWS)
    col = np.arange(GRID_W)
    c0 = np.clip(col - NA_WIN_W // 2, 0, GRID_W - NA_WIN_W)
    col_ok = (col[None, :] >= c0[:, None]) & (col[None, :] < c0[:, None] + NA_WIN_W)
    row_idx, ok = [], []
    for g in (0, 1, NA_N_GROUPS - 1):
        r = NA_GROUP_ROWS * g + ri
        r0 = np.clip(r - NA_WIN_H // 2, 0, GRID_H - NA_WIN_H)
        kr = _na_band_start(g) + bi
        row_ok = (kr[None, :] >= r0[:, None]) & (kr[None, :] < r0[:, None] + NA_WIN_H)
        row_idx.append(np.clip(kr[None, :] - r[:, None] + NA_WIN_H - 1, 0, 2 * NA_WIN_H - 2))
        ok.append(row_ok[:, None, :, None] & col_ok[None, :, None, :])
    n_rows = 3 * NA_GROUP_ROWS * NA_BAND_ROWS
    by_row = jnp.take(rpb, np.stack(row_idx).reshape(-1), axis=1)
    pad = GRID_W - NA_WIN_W
    f = jnp.pad(by_row, ((0, 0), (0, 0), (pad, pad)))
    t = jnp.stack([f[:, :, GRID_W - 1 - qc:2 * GRID_W - 1 - qc] for qc in range(GRID_W)], axis=2)
    t = t.reshape(NA_HEADS, 3, NA_GROUP_ROWS, NA_BAND_ROWS, GRID_W, GRID_W).transpose(0, 1, 2, 4, 3, 5)
    t = jnp.where(np.stack(ok)[None], t * LOG2E, -jnp.inf)
    return t.reshape(NA_HEADS, 3, NA_GROUP_ROWS * GRID_W, NA_BAND_ROWS * GRID_W)


def _na_attention(q, k, v, kc, vc, bias):
    head_block = lambda b, h: (b, h)
    return pl.pallas_call(
        _na_kernel,
        grid=(BATCH, NA_HEADS),
        in_specs=[
            pl.BlockSpec((SEQ, HEAD_DIM), head_block),
            pl.BlockSpec((SEQ, HEAD_DIM), head_block),
            pl.BlockSpec((SEQ, HEAD_DIM), head_block),
            pl.BlockSpec((CTX_LEN, HEAD_DIM), head_block),
            pl.BlockSpec((CTX_LEN, HEAD_DIM), head_block),
            pl.BlockSpec((None,) + bias.shape[1:], lambda b, h: (h, 0, 0, 0)),
        ],
        out_specs=pl.BlockSpec((SEQ, HEAD_DIM), head_block),
        out_shape=jax.ShapeDtypeStruct((BATCH * SEQ, NA_WIDTH), BF),
        compiler_params=_params(2),
        name="na_attention",
    )(q, k, v, kc, vc, bias)


def _head_gain(g, width):
    return jnp.tile(g, width // g.shape[0]).reshape(1, width)


def _ffn_up_kernel(*refs, cast_w2):
    if cast_w2:
        x_ref, wg_ref, wu_ref, w2_ref, o_ref, w2_bf_ref = refs
        w2_bf_ref[...] = w2_ref[...].astype(BF)
    else:
        x_ref, wg_ref, wu_ref, o_ref = refs
    x = x_ref[...]
    gate = jnp.dot(x, wg_ref[...].astype(BF), preferred_element_type=F32)
    up = jnp.dot(x, wu_ref[...].astype(BF), preferred_element_type=F32)
    o_ref[...] = (gate * jax.nn.sigmoid(gate) * up).astype(o_ref.dtype)


def _ffn_up(hn, w1, w2, layer, f, tok, cast_w2):
    tm = 2048 if tok.latent else tok.rows
    tn = 256
    ni, nj = tok.rows // tm, D_FF // tn
    in_specs = [
        pl.BlockSpec((tm, D_MODEL), lambda i, j: (i, 0), pipeline_mode=pl.Buffered(1)),
        pl.BlockSpec((None, None, D_MODEL, tn), lambda i, j: (layer, f, 0, j)),
        pl.BlockSpec((None, None, D_MODEL, tn), lambda i, j: (layer, f, 0, nj + j)),
    ]
    out_specs = [pl.BlockSpec((tm, tn), lambda i, j: (i, j))]
    out_shape = [jax.ShapeDtypeStruct((tok.rows, D_FF), BF)]
    operands = [hn, w1, w1]
    if cast_w2:
        assert D_FF % (ni * nj) == 0
        slab = D_FF // (ni * nj)
        in_specs.append(pl.BlockSpec((None, None, slab, D_MODEL), lambda i, j: (layer, f, i * nj + j, 0)))
        out_specs.append(pl.BlockSpec((slab, D_MODEL), lambda i, j: (i * nj + j, 0)))
        out_shape.append(jax.ShapeDtypeStruct((D_FF, D_MODEL), BF))
        operands.append(w2)
    return pl.pallas_call(
        functools.partial(_ffn_up_kernel, cast_w2=cast_w2),
        grid=(ni, nj),
        in_specs=in_specs,
        out_specs=out_specs,
        out_shape=out_shape,
        compiler_params=_params(2),
        name="ffn_up",
    )(*operands)


def _ffn(h, g, mt, layer, f, w1, w2, tok, w2_bf=None):
    chunk = 6 * f
    hn = _normmod(h, g, mt, layer, chunk, tok)
    if w2_bf is None:
        u, w2_bf = _ffn_up(hn, w1, w2, layer, f, tok, True)
    else:
        u, = _ffn_up(hn, w1, w2, layer, f, tok, False)
    tn = 512
    tile = pl.BlockSpec((tok.tm, tn), lambda i, j: (i, j))
    out, = _matmul(
        "ffn_down", u, [(w2_bf, (), 0)], [(h, tile), (mt, _mod_spec(tok, layer, chunk + 2, tn))],
        [((tok.rows, D_MODEL), F32, tile)], _epi_residual(0.5),
        rows=tok.rows, tm=tok.tm, tn=tn, n_cols=D_MODEL, lhs_single=True)
    return out, w2_bf


def _out_proj(name, o, w, prefix, h, mt, layer, tok):
    tn = 512
    tile = pl.BlockSpec((tok.tm, tn), lambda i, j: (i, j))
    out, = _matmul(
        name, o, [(w, prefix, 0)], [(h, tile), (mt, _mod_spec(tok, layer, 5, tn))],
        [((tok.rows, D_MODEL), F32, tile)], _epi_residual(1.0),
        rows=tok.rows, tm=tok.tm, tn=tn, n_cols=D_MODEL)
    return out


def _head_proj(name, u, w, prefix, col_off, width, kinds, gain, tabs, shift, tok, tn=512):
    out, = _matmul(
        name, u, [(w, prefix, col_off // tn)],
        [(gain, pl.BlockSpec((1, tn), lambda i, j: (0, j)))] + _rope_extras(tabs, tok),
        [((tok.rows, width), BF, pl.BlockSpec((tok.tm, tn), lambda i, j: (i, j)))],
        _epi_heads(kinds * (tn // (LANES * len(kinds))), shift),
        rows=tok.rows, tm=tok.tm, tn=tn, n_cols=width, lagged="NR" in kinds)
    return out


def _even_projections(u, w_in, j, gains, tok, want_q):
    na_q_g, na_k_g, gq_q_g, gq_k_g = gains
    scale = HEAD_DIM ** -0.5 * LOG2E
    rot = "NR" if tok.latent else "N"
    tabs = _rope128_tables() if tok.latent else []
    ones = jnp.ones((HEAD_DIM,), F32)
    proj = functools.partial(_head_proj, u=u, w=w_in, prefix=(j,), shift=_ROPE128_SHIFT, tok=tok)
    out = {}
    if want_q:
        out["na_q"] = proj("na_q", col_off=0, width=NA_WIDTH, kinds=("N",),
                           gain=_head_gain(na_q_g * scale, NA_WIDTH), tabs=[])
        out["gq_q"] = proj("gq_q", col_off=NA_WIDTH, width=GQA_Q_WIDTH, kinds=(rot,),
                           gain=_head_gain(gq_q_g * scale, GQA_Q_WIDTH), tabs=tabs)
    out["na_k"] = proj("na_k", col_off=EV_Q_COLS, width=NA_WIDTH, kinds=("N",),
                       gain=_head_gain(na_k_g, NA_WIDTH), tabs=[])
    out["na_v"] = proj("na_v", col_off=EV_Q_COLS + NA_WIDTH, width=NA_WIDTH, kinds=("P",),
                       gain=_head_gain(ones, NA_WIDTH), tabs=[])
    out["gq_k"] = proj("gq_k", col_off=EV_Q_COLS + 2 * NA_WIDTH, width=GQA_KV_WIDTH, kinds=(rot,),
                       gain=_head_gain(gq_k_g, GQA_KV_WIDTH), tabs=tabs)
    out["gq_v"] = proj("gq_v", col_off=EV_Q_COLS + 2 * NA_WIDTH + GQA_KV_WIDTH, width=GQA_KV_WIDTH,
                       kinds=("P",), gain=_head_gain(ones, GQA_KV_WIDTH), tabs=[])
    return out


def _even_mixer(ul, uc, hl, hc, mt, layer, w_in, w_out, j, gains, rpb, lat, ctx, ctx_out):
    pl_ = _even_projections(ul, w_in, j, gains, lat, True)
    pc = _even_projections(uc, w_in, j, gains, ctx, ctx_out)
    head = lambda h: h
    a = _na_attention(pl_["na_q"], pl_["na_k"], pl_["na_v"], pc["na_k"], pc["na_v"], _na_bias_tables(rpb))
    gq_segs = [(CTX_LEN, [(pc["gq_k"], head)], (pc["gq_v"], head)),
               (SEQ, [(pl_["gq_k"], head)], (pl_["gq_v"], head))]
    b = _attention("gqa", pl_["gq_q"], gq_segs, q_rows=SEQ, n_kv=GQA_KV_HEADS, group=GQA_GROUP,
                   dk=HEAD_DIM, dv=HEAD_DIM, tq=128)
    hl = _out_proj("ev_out", jnp.concatenate([a, b], axis=-1), w_out, (j,), hl, mt, layer, lat)
    if not ctx_out:
        return hl, hc
    ac = _attention("na_ctx", pc["na_q"], [(CTX_LEN, [(pc["na_k"], head)], (pc["na_v"], head))],
                    q_rows=CTX_LEN, n_kv=NA_HEADS, group=1, dk=HEAD_DIM, dv=HEAD_DIM, tq=CTX_LEN)
    bc = _attention("gqa_ctx", pc["gq_q"], gq_segs[:1], q_rows=CTX_LEN, n_kv=GQA_KV_HEADS, group=GQA_GROUP,
                    dk=HEAD_DIM, dv=HEAD_DIM, tq=128)
    hc = _out_proj("ev_out_ctx", jnp.concatenate([ac, bc], axis=-1), w_out, (j,), hc, mt, layer, ctx)
    return hl, hc


def _mla_down(u, w_down_p, gains, tok):
    q_a_g, kv_a_g, kr_g = gains
    tabs = _rope64_tables() if tok.latent else []
    tm = 512
    t = _Tokens(tok.rows, tm, tok.latent)
    full = lambda n: pl.BlockSpec((1, n), lambda i, j: (0, 0))
    rows = lambda n: pl.BlockSpec((tm, n), lambda i, j: (i, 0))
    return _matmul(
        "mla_down", u, [(w_down_p, (), 0)],
        [(q_a_g.reshape(1, -1), full(MLA_Q_RANK)), (kv_a_g.reshape(1, -1), full(MLA_KV_RANK)),
         (kr_g, full(LANES))] + _rope_extras(tabs, t),
        [((tok.rows, MLA_Q_RANK), BF, rows(MLA_Q_RANK)), ((tok.rows, MLA_KV_RANK), BF, rows(MLA_KV_RANK)),
         ((tok.rows, LANES), BF, rows(LANES))],
        _epi_mla_down(tok.latent), rows=tok.rows, tm=tm, tn=MLA_DOWN_COLS, n_cols=MLA_DOWN_COLS,
        groups=_MLA_DOWN_GROUPS)


def _rope_slot_gain(g, rotated):
    return jnp.concatenate([g, g if rotated else jnp.zeros_like(g)])


def _mla_mixer(ul, uc, hl, hc, mt, layer, w, g, lat, ctx, ctx_out):
    scale = MLA_QK ** -0.5 * LOG2E
    kv_gain = _head_gain(jnp.concatenate([g["kn"], jnp.ones((MLA_V,), F32)]), MLA_HEADS * (MLA_NOPE + MLA_V))
    tabs = _rope64_tables()

    def down(u, tok):
        kr_gain = _rope_slot_gain(g["kr"], tok.latent).reshape(1, LANES)
        return _mla_down(u, w["down"], (g["q_a"], g["kv_a"], kr_gain), tok)

    def keys_values(ckv, tok):
        return _head_proj("mla_kv", ckv, w["ukv"], (), 0, MLA_HEADS * (MLA_NOPE + MLA_V), ("N", "P"),
                          kv_gain, [], 0, tok)

    def queries(name, cq, tok):
        gain = jnp.concatenate([g["qn"], _rope_slot_gain(g["qr"], tok.latent)]) * scale
        return _head_proj(name, cq, w["uq"], (), 0, MLA_HEADS * MLA_Q_SLOT,
                          ("N", "NR" if tok.latent else "N"), _head_gain(gain, MLA_HEADS * MLA_Q_SLOT),
                          tabs if tok.latent else [], _ROPE64_SHIFT, tok)

    cq_l, ckv_l, kr_l = down(ul, lat)
    cq_c, ckv_c, kr_c = down(uc, ctx)
    kv_l = keys_values(ckv_l, lat)
    kv_c = keys_values(ckv_c, ctx)
    q_l = queries("mla_q", cq_l, lat)
    nope = lambda h: 2 * h
    val = lambda h: 2 * h + 1
    shared = lambda h: 0
    segs = [(CTX_LEN, [(kv_c, nope), (kr_c, shared)], (kv_c, val)),
            (SEQ, [(kv_l, nope), (kr_l, shared)], (kv_l, val))]
    o = _attention("mla", q_l, segs, q_rows=SEQ, n_kv=MLA_HEADS, group=1, dk=MLA_Q_SLOT, dv=MLA_V, tq=512)
    hl = _out_proj("mla_out", o, w["o"], (), hl, mt, layer, lat)
    if not ctx_out:
        return hl, hc
    q_c = queries("mla_q_ctx", cq_c, ctx)
    oc = _attention("mla_ctx", q_c, segs[:1], q_rows=CTX_LEN, n_kv=MLA_HEADS, group=1, dk=MLA_Q_SLOT,
                    dv=MLA_V, tq=CTX_LEN)
    hc = _out_proj("mla_out_ctx", oc, w["o"], (), hc, mt, layer, ctx)
    return hl, hc


def _mla_weights(w_down, w_uq, w_ukv, w_o, j):
    down = jnp.concatenate([w_down[j], w_down[j][:, -MLA_ROPE:]], axis=1).astype(BF)
    uq = w_uq[j].reshape(MLA_Q_RANK, MLA_HEADS, MLA_QK)
    uq = jnp.concatenate([uq, uq[:, :, -MLA_ROPE:]], axis=2).astype(BF)
    return {"down": down, "uq": uq.reshape(MLA_Q_RANK, MLA_HEADS * MLA_Q_SLOT), "ukv": w_ukv[j], "o": w_o[j]}


def kernel(x, c, ctx, c_ctx, norm_g, w_mod, b_mod, ffn_w1, ffn_w2, ev_w_in, ev_w_out, na_q_g, na_k_g, na_rpb, gq_q_g, gq_k_g, mla_w_down, mla_q_a_g, mla_kv_a_g, mla_w_uq, mla_w_ukv, mla_qn_g, mla_qr_g, mla_kn_g, mla_kr_g, mla_w_o):
    lat = _Tokens(BATCH * SEQ, 1024, True)
    ctk = _Tokens(BATCH * CTX_LEN, 512, False)
    cvec = jnp.zeros((MOD_ROWS, D_MODEL), F32).at[:BATCH].set(c).at[CTX_MOD_ROW].set(c_ctx)
    mt = _modulation(cvec, w_mod, b_mod).reshape(DEPTH, MOD_ROWS, 1, N_MOD * D_MODEL)
    hl = x.reshape(BATCH * SEQ, D_MODEL)
    hc = ctx.reshape(BATCH * CTX_LEN, D_MODEL)
    for i in range(DEPTH):
        last = i == DEPTH - 1
        g = norm_g[i]
        hl, w2_bf = _ffn(hl, g[0], mt, i, 0, ffn_w1, ffn_w2, lat)
        hc, _ = _ffn(hc, g[0], mt, i, 0, ffn_w1, ffn_w2, ctk, w2_bf)
        ul = _normmod(hl, g[1], mt, i, 3, lat)
        uc = _normmod(hc, g[1], mt, i, 3, ctk)
        j = i // 2
        if i % 2 == 0:
            gains = (na_q_g[j], na_k_g[j], gq_q_g[j], gq_k_g[j])
            hl, hc = _even_mixer(ul, uc, hl, hc, mt, i, ev_w_in, ev_w_out, j, gains, na_rpb[j], lat, ctk,
                                 not last)
        else:
            w = _mla_weights(mla_w_down, mla_w_uq, mla_w_ukv, mla_w_o, j)
            gains = {"q_a": mla_q_a_g[j], "kv_a": mla_kv_a_g[j], "qn": mla_qn_g[j], "qr": mla_qr_g[j],
                     "kn": mla_kn_g[j], "kr": mla_kr_g[j]}
            hl, hc = _mla_mixer(ul, uc, hl, hc, mt, i, w, gains, lat, ctk, not last)
        hl, w2_bf = _ffn(hl, g[2], mt, i, 1, ffn_w1, ffn_w2, lat)
        if not last:
            hc, _ = _ffn(hc, g[2], mt, i, 1, ffn_w1, ffn_w2, ctk, w2_bf)
    return hl.reshape(BATCH, SEQ, D_MODEL)
```

```python
import functools

import numpy as np
import jax
import jax.numpy as jnp
from jax import lax
from jax.experimental import pallas as pl
from jax.experimental.pallas import tpu as pltpu

D_MODEL = 4096
BATCH = 2
SEQ = 4096
DEPTH = 2
CTX_LEN = 256
GRID_W = 64
GRID_H = SEQ // GRID_W
HEAD_DIM = 128
EPS = 1e-6
ROPE_THETA = 10000.0
N_MOD = 9
D_FF = 10240
NA_HEADS = 16
NA_WIN_H = 8
NA_WIN_W = 16
GQA_Q_HEADS = 16
GQA_KV_HEADS = 4
GQA_GROUP = GQA_Q_HEADS // GQA_KV_HEADS
NA_WIDTH = NA_HEADS * HEAD_DIM
GQA_Q_WIDTH = GQA_Q_HEADS * HEAD_DIM
GQA_KV_WIDTH = GQA_KV_HEADS * HEAD_DIM
EV_Q_COLS = NA_WIDTH + GQA_Q_WIDTH
MLA_HEADS = 32
MLA_Q_RANK = 1024
MLA_KV_RANK = 512
MLA_NOPE = 128
MLA_ROPE = 64
MLA_V = 128
MLA_QK = MLA_NOPE + MLA_ROPE

LANES = 128
MLA_Q_SLOT = 2 * LANES
MLA_DOWN_COLS = MLA_Q_RANK + MLA_KV_RANK + LANES
MOD_ROWS = 8
CTX_MOD_ROW = BATCH

NA_GROUP_ROWS = 4
NA_BAND_ROWS = 12
NA_N_GROUPS = GRID_H // NA_GROUP_ROWS

ATTN_KEY_CHUNK = 256
LOG2E = 1.4426950408889634

VMEM_LIMIT = 56 * 1024 * 1024

BF = jnp.bfloat16
F32 = jnp.float32


def _params(n_axes):
    return pltpu.CompilerParams(dimension_semantics=("arbitrary",) * n_axes, vmem_limit_bytes=VMEM_LIMIT)


def _mod_kernel(c_ref, w_ref, b_ref, o_ref):
    c = c_ref[...]
    a = (c * jax.nn.sigmoid(c)).astype(BF)
    o_ref[...] = jnp.dot(a, w_ref[...].astype(BF), preferred_element_type=F32) + b_ref[...]


def _modulation(cvec, w_mod, b_mod):
    n = N_MOD * D_MODEL
    tn = 512
    return pl.pallas_call(
        _mod_kernel,
        grid=(DEPTH, n // tn),
        in_specs=[
            pl.BlockSpec((MOD_ROWS, D_MODEL), lambda l, j: (0, 0)),
            pl.BlockSpec((None, D_MODEL, tn), lambda l, j: (l, 0, j)),
            pl.BlockSpec((None, 1, tn), lambda l, j: (l, 0, j)),
        ],
        out_specs=pl.BlockSpec((None, MOD_ROWS, tn), lambda l, j: (l, 0, j)),
        out_shape=jax.ShapeDtypeStruct((DEPTH, MOD_ROWS, n), F32),
        compiler_params=_params(2),
        name="modulation",
    )(cvec, w_mod, b_mod.reshape(DEPTH, 1, n))


class _Tokens:
    def __init__(self, rows, tm, latent):
        self.rows, self.tm, self.latent = rows, tm, latent

    def mod_row(self, i):
        return i // (SEQ // self.tm) if self.latent else CTX_MOD_ROW

    def pos_block(self, i):
        return i % (SEQ // self.tm)


def _mod_spec(tok, layer, chunk, tn):
    return pl.BlockSpec((None, None, 1, tn),
                        lambda i, j: (layer, tok.mod_row(i), 0, chunk * (D_MODEL // tn) + j))


def _normmod_kernel(x_ref, g_ref, sh_ref, sc_ref, o_ref):
    x = x_ref[...]
    ms = jnp.mean(x * x, axis=-1, keepdims=True)
    y = x * lax.rsqrt(ms + EPS) * g_ref[...]
    o_ref[...] = (y * (1.0 + sc_ref[...]) + sh_ref[...]).astype(o_ref.dtype)


def _normmod(x, g, mt, layer, chunk, tok):
    tm = 256
    t = _Tokens(tok.rows, tm, tok.latent)
    sh = _mod_spec(t, layer, chunk, D_MODEL)
    sc = _mod_spec(t, layer, chunk + 1, D_MODEL)
    return pl.pallas_call(
        _normmod_kernel,
        grid=(tok.rows // tm,),
        in_specs=[
            pl.BlockSpec((tm, D_MODEL), lambda i: (i, 0)),
            pl.BlockSpec((1, D_MODEL), lambda i: (0, 0)),
            pl.BlockSpec(sh.block_shape, lambda i: sh.index_map(i, 0)),
            pl.BlockSpec(sc.block_shape, lambda i: sc.index_map(i, 0)),
        ],
        out_specs=pl.BlockSpec((tm, D_MODEL), lambda i: (i, 0)),
        out_shape=jax.ShapeDtypeStruct((tok.rows, D_MODEL), BF),
        compiler_params=_params(1),
        name="normmod",
    )(x, g.reshape(1, D_MODEL), mt, mt)


def _mm_kernel(*refs, n_rhs, n_extra, n_out, groups, epilogue):
    lhs_ref = refs[0]
    rhs_refs = refs[1:1 + n_rhs]
    extra = refs[1 + n_rhs:1 + n_rhs + n_extra]
    out_refs = refs[1 + n_rhs + n_extra:]
    acc_refs = out_refs[n_out:]
    out_refs = out_refs[:n_out]
    a = lhs_ref[...]
    if not acc_refs:
        for c0, c1 in groups:
            accs = [jnp.dot(a, r[:, c0:c1].astype(BF), preferred_element_type=F32) for r in rhs_refs]
            epilogue(c0, c1, accs, extra, out_refs)
        return

    @pl.when(pl.program_id(0) == 0)
    def _():
        for acc in acc_refs:
            acc[...] = jnp.zeros_like(acc)

    for c0, c1 in groups:
        epilogue(c0, c1, [acc[:, c0:c1] for acc in acc_refs], extra, out_refs)
    for c0, c1 in groups:
        for r, acc in zip(rhs_refs, acc_refs):
            acc[:, c0:c1] = jnp.dot(a, r[:, c0:c1].astype(BF), preferred_element_type=F32)


MXU_COLS = 256


def _matmul(name, lhs, rhs, extras, outs, epilogue, *, rows, tm, tn, n_cols, lhs_single=False, groups=None,
            lagged=False):
    if groups is None:
        groups = tuple((c, c + MXU_COLS) for c in range(0, tn, MXU_COLS))
    kdim = lhs.shape[1]
    ni, nj = rows // tm, n_cols // tn
    if lagged:
        n_tiles = ni * nj
        grid = (n_tiles + 1,)
        cur = lambda f: (lambda t: f(*divmod(jnp.minimum(t, n_tiles - 1), nj)))
        prev = lambda f: (lambda t: f(*divmod(jnp.maximum(t - 1, 0), nj)))
        scratch = [pltpu.VMEM((tm, tn), F32) for _ in rhs]
    else:
        grid = (ni, nj)
        cur = prev = lambda f: f
        scratch = []
    lhs_mode = pl.Buffered(1) if lhs_single else None
    in_specs = [pl.BlockSpec((tm, kdim), cur(lambda i, j: (i, 0)), pipeline_mode=lhs_mode)]
    for _, prefix, off in rhs:
        in_specs.append(pl.BlockSpec((None,) * len(prefix) + (kdim, tn),
                                     cur(lambda i, j, prefix=prefix, off=off: prefix + (0, off + j))))
    in_specs += [pl.BlockSpec(spec.block_shape, prev(spec.index_map)) for _, spec in extras]
    kern = functools.partial(_mm_kernel, n_rhs=len(rhs), n_extra=len(extras), n_out=len(outs),
                             groups=groups, epilogue=epilogue)
    return pl.pallas_call(
        kern,
        grid=grid,
        in_specs=in_specs,
        out_specs=[pl.BlockSpec(spec.block_shape, prev(spec.index_map)) for _, _, spec in outs],
        out_shape=[jax.ShapeDtypeStruct(shape, dtype) for shape, dtype, _ in outs],
        scratch_shapes=scratch,
        compiler_params=_params(len(grid)),
        name=name,
    )(lhs, *[r[0] for r in rhs], *[e[0] for e in extras])


def _epi_residual(coef):
    def epi(c0, c1, accs, extra, outs):
        res_ref, gate_ref = extra
        outs[0][:, c0:c1] = res_ref[:, c0:c1] + (coef * gate_ref[:, c0:c1]) * accs[0]
    return epi


def _rms_chunk(x, gain):
    ms = jnp.mean(x * x, axis=-1, keepdims=True)
    return x * lax.rsqrt(ms + EPS) * gain


def _rotate(x, tabs, shift):
    return x * tabs[0][...] + pltpu.roll(x, shift, 1) * tabs[1][...]


def _epi_heads(kinds, shift):
    def epi(c0, c1, accs, extra, outs):
        gain_ref, tabs = extra[0], extra[1:]
        for c in range(c0, c1, LANES):
            kind = kinds[c // LANES]
            x = accs[0][:, c - c0:c - c0 + LANES]
            if kind != "P":
                x = _rms_chunk(x, gain_ref[:, c:c + LANES])
            if kind == "NR":
                x = _rotate(x, tabs, shift)
            outs[0][:, c:c + LANES] = x.astype(outs[0].dtype)
    return epi


_MLA_DOWN_GROUPS = ((0, MLA_Q_RANK), (MLA_Q_RANK, MLA_Q_RANK + MLA_KV_RANK),
                    (MLA_Q_RANK + MLA_KV_RANK, MLA_DOWN_COLS))


def _epi_mla_down(rope):
    def epi(c0, c1, accs, extra, outs):
        which = [g[0] for g in _MLA_DOWN_GROUPS].index(c0)
        x = _rms_chunk(accs[0], extra[which][...])
        if rope and which == 2:
            x = _rotate(x, extra[3:], _ROPE64_SHIFT)
        outs[which][...] = x.astype(outs[which].dtype)
    return epi


def _axial_angles(rot_dim):
    t = jnp.arange(SEQ)
    row = (t // GRID_W).astype(F32)
    col = (t % GRID_W).astype(F32)
    axis_dim = rot_dim // 2
    inv_freq = ROPE_THETA ** (-jnp.arange(0, axis_dim, 2, dtype=F32) / axis_dim)
    ang = jnp.concatenate([row[:, None] * inv_freq, col[:, None] * inv_freq], axis=-1)
    return jnp.cos(ang), jnp.sin(ang)


_ROPE128_SHIFT = HEAD_DIM // 2
_ROPE64_SHIFT = LANES - MLA_ROPE // 2


def _rope128_tables():
    cos, sin = _axial_angles(HEAD_DIM)
    return [jnp.concatenate([cos, cos], -1), jnp.concatenate([-sin, sin], -1)]


def _rope64_tables():
    cos, sin = _axial_angles(MLA_ROPE)
    z64 = jnp.zeros((SEQ, LANES - MLA_ROPE), F32)
    return [jnp.concatenate([cos, cos, z64], -1), jnp.concatenate([-sin, sin, z64], -1)]


def _rope_extras(tabs, tok):
    return [(t, pl.BlockSpec((tok.tm, LANES), lambda i, j: (tok.pos_block(i), 0))) for t in tabs]


def _attn_kernel(*refs, n_seg, n_parts, group, dk, dv, tq, seg_lens, chunk):
    q_ref = refs[0]
    k_refs = refs[1:1 + n_seg * n_parts]
    v_refs = refs[1 + n_seg * n_parts:1 + n_seg * n_parts + n_seg]
    o_ref, k_scr, v_scr = refs[1 + n_seg * n_parts + n_seg:]

    @pl.when(pl.program_id(2) == 0)
    def _():
        off = 0
        for s in range(n_seg):
            length = seg_lens[s]
            col = 0
            for part in k_refs[s * n_parts:(s + 1) * n_parts]:
                width = part.shape[-1]
                k_scr[off:off + length, col:col + width] = part[...]
                col += width
            v_scr[off:off + length, :] = v_refs[s][...]
            off += length

    q = q_ref[...]
    if group > 1:
        q = jnp.concatenate([q[:, g * dk:(g + 1) * dk] for g in range(group)], axis=0)
    total = sum(seg_lens)
    m = l = acc = None
    for c0 in range(0, total, chunk):
        c1 = min(c0 + chunk, total)
        s = lax.dot_general(q, k_scr[c0:c1, :], (((1,), (1,)), ((), ())), preferred_element_type=F32)
        mc = jnp.max(s, axis=-1, keepdims=True)
        if m is None:
            m_new = mc
        else:
            m_new = jnp.maximum(m, mc)
            alpha = jnp.exp2(m - m_new)
        p = jnp.exp2(s - m_new)
        pv = jnp.dot(p.astype(BF), v_scr[c0:c1, :], preferred_element_type=F32)
        ps = jnp.sum(p, axis=-1, keepdims=True)
        if m is None:
            l, acc = ps, pv
        else:
            l, acc = alpha * l + ps, alpha * acc + pv
        m = m_new
    o = acc / l
    for g in range(group):
        o_ref[:, g * dv:(g + 1) * dv] = o[g * tq:(g + 1) * tq].astype(o_ref.dtype)


def _attention(name, q, segs, *, q_rows, n_kv, group, dk, dv, tq):
    nq = q_rows // tq
    n_parts = len(segs[0][1])
    in_specs = [pl.BlockSpec((tq, group * dk), lambda b, h, i: (b * nq + i, h))]
    operands = [q]
    for length, k_parts, _ in segs:
        for arr, col in k_parts:
            in_specs.append(pl.BlockSpec((length, LANES), lambda b, h, i, col=col: (b, col(h))))
            operands.append(arr)
    for length, _, (arr, col) in segs:
        in_specs.append(pl.BlockSpec((length, dv), lambda b, h, i, col=col: (b, col(h))))
        operands.append(arr)
    total = sum(s[0] for s in segs)
    kern = functools.partial(_attn_kernel, n_seg=len(segs), n_parts=n_parts, group=group, dk=dk, dv=dv,
                             tq=tq, seg_lens=tuple(s[0] for s in segs), chunk=ATTN_KEY_CHUNK)
    return pl.pallas_call(
        kern,
        grid=(BATCH, n_kv, nq),
        in_specs=in_specs,
        out_specs=pl.BlockSpec((tq, group * dv), lambda b, h, i: (b * nq + i, h)),
        out_shape=jax.ShapeDtypeStruct((BATCH * q_rows, n_kv * group * dv), BF),
        scratch_shapes=[pltpu.VMEM((total, dk), BF), pltpu.VMEM((total, dv), BF)],
        compiler_params=_params(3),
        name=name,
    )(*operands)


def _na_band_start(g):
    return min(max(NA_GROUP_ROWS * g - NA_WIN_H // 2, 0), GRID_H - NA_BAND_ROWS)


def _na_variant(g):
    return min(g, 1) + (g == NA_N_GROUPS - 1)


def _na_kernel(q_ref, k_ref, v_ref, kc_ref, vc_ref, bias_ref, o_ref):
    tq = NA_GROUP_ROWS * GRID_W
    band = NA_BAND_ROWS * GRID_W
    nt = (((1,), (1,)), ((), ()))
    kc = kc_ref[...]
    vc = vc_ref[...]
    for g in range(NA_N_GROUPS):
        start = _na_band_start(g) * GRID_W
        q = q_ref[g * tq:(g + 1) * tq, :]
        kb = k_ref[start:start + band, :]
        vb = v_ref[start:start + band, :]
        s_b = lax.dot_general(q, kb, nt, preferred_element_type=F32) + bias_ref[_na_variant(g)]
        s_c = lax.dot_general(q, kc, nt, preferred_element_type=F32)
        m = jnp.maximum(jnp.max(s_b, axis=-1, keepdims=True), jnp.max(s_c, axis=-1, keepdims=True))
        p_b = jnp.exp2(s_b - m)
        p_c = jnp.exp2(s_c - m)
        l = jnp.sum(p_b, axis=-1, keepdims=True) + jnp.sum(p_c, axis=-1, keepdims=True)
        o = (jnp.dot(p_b.astype(BF), vb, preferred_element_type=F32)
             + jnp.dot(p_c.astype(BF), vc, preferred_element_type=F32))
        o_ref[g * tq:(g + 1) * tq, :] = (o / l).astype(o_ref.dtype)


def _na_band_rows():
    ri = np.arange(NA_GROUP_ROWS)
    bi = np.arange(NA_BAND_ROWS)
    row_idx, row_ok = [], []
    for g in (0, 1, NA_N_GROUPS - 1):
        r = NA_GROUP_ROWS * g + ri
        r0 = np.clip(r - NA_WIN_H // 2, 0, GRID_H - NA_WIN_H)
        kr = _na_band_start(g) + bi
        row_ok.append((kr[None, :] >= r0[:, None]) & (kr[None, :] < r0[:, None] + NA_WIN_H))
        row_idx.append(np.clip(kr[None, :] - r[:, None] + NA_WIN_H - 1, 0, 2 * NA_WIN_H - 2))
    return np.stack(row_idx), np.stack(row_ok)


def _na_bias_kernel(f_ref, o_ref):
    _, row_ok = _na_band_rows()
    lane = lax.broadcasted_iota(jnp.int32, (GRID_W, LANES), 1)
    qc = lax.broadcasted_iota(jnp.int32, (GRID_W, LANES), 0)
    kc = lane % GRID_W
    left = lane < GRID_W
    c0 = jnp.clip(qc - NA_WIN_W // 2, 0, GRID_W - NA_WIN_W)
    col_ok = (kc >= c0) & (kc < c0 + NA_WIN_W)
    for v in range(3):
        for ri in range(NA_GROUP_ROWS):
            for bp in range(NA_BAND_ROWS // 2):
                halves = []
                for half in range(2):
                    row = (v * NA_GROUP_ROWS + ri) * NA_BAND_ROWS + 2 * bp + half
                    fb = jnp.broadcast_to(f_ref[row:row + 1, :], (GRID_W, LANES))
                    shift = (GRID_W * half - (GRID_W - 1)) % LANES
                    halves.append(pltpu.roll(fb, shift, 1, stride=1, stride_axis=0))
                ok_l, ok_r = bool(row_ok[v, ri, 2 * bp]), bool(row_ok[v, ri, 2 * bp + 1])
                ok = col_ok & ((left & ok_l) | (~left & ok_r))
                tile = jnp.where(left, halves[0], halves[1]) * LOG2E
                o_ref[v, ri * GRID_W:(ri + 1) * GRID_W, bp * LANES:(bp + 1) * LANES] = (
                    jnp.where(ok, tile, -jnp.inf))


def _na_bias_tables(rpb):
    row_idx, _ = _na_band_rows()
    n_rows = row_idx.size
    by_row = jnp.take(rpb, row_idx.reshape(-1), axis=1)
    pad = GRID_W - NA_WIN_W
    f = jnp.pad(by_row, ((0, 0), (0, 0), (pad, LANES - pad - by_row.shape[-1])))
    shape = (3, NA_GROUP_ROWS * GRID_W, NA_BAND_ROWS * GRID_W)
    return pl.pallas_call(
        _na_bias_kernel,
        grid=(NA_HEADS,),
        in_specs=[pl.BlockSpec((None, n_rows, LANES), lambda h: (h, 0, 0))],
        out_specs=pl.BlockSpec((None,) + shape, lambda h: (h, 0, 0, 0)),
        out_shape=jax.ShapeDtypeStruct((NA_HEADS,) + shape, F32),
        compiler_params=_params(1),
        name="na_bias",
    )(f)


def _na_attention(q, k, v, kc, vc, bias):
    head_block = lambda b, h: (b, h)
    return pl.pallas_call(
        _na_kernel,
        grid=(BATCH, NA_HEADS),
        in_specs=[
            pl.BlockSpec((SEQ, HEAD_DIM), head_block),
            pl.BlockSpec((SEQ, HEAD_DIM), head_block),
            pl.BlockSpec((SEQ, HEAD_DIM), head_block),
            pl.BlockSpec((CTX_LEN, HEAD_DIM), head_block),
            pl.BlockSpec((CTX_LEN, HEAD_DIM), head_block),
            pl.BlockSpec((None,) + bias.shape[1:], lambda b, h: (h, 0, 0, 0)),
        ],
        out_specs=pl.BlockSpec((SEQ, HEAD_DIM), head_block),
        out_shape=jax.ShapeDtypeStruct((BATCH * SEQ, NA_WIDTH), BF),
        compiler_params=_params(2),
        name="na_attention",
    )(q, k, v, kc, vc, bias)


def _head_gain(g, width):
    return jnp.tile(g, width // g.shape[0]).reshape(1, width)


def _ffn_up_kernel(*refs, cast_w2):
    if cast_w2:
        x_ref, wg_ref, wu_ref, w2_ref, o_ref, w2_bf_ref = refs
        w2_bf_ref[...] = w2_ref[...].astype(BF)
    else:
        x_ref, wg_ref, wu_ref, o_ref = refs
    x = x_ref[...]
    gate = jnp.dot(x, wg_ref[...].astype(BF), preferred_element_type=F32)
    up = jnp.dot(x, wu_ref[...].astype(BF), preferred_element_type=F32)
    o_ref[...] = (gate * jax.nn.sigmoid(gate) * up).astype(o_ref.dtype)


def _ffn_up(hn, w1, w2, layer, f, tok, cast_w2):
    tm = 2048 if tok.latent else tok.rows
    tn = 256
    ni, nj = tok.rows // tm, D_FF // tn
    in_specs = [
        pl.BlockSpec((tm, D_MODEL), lambda i, j: (i, 0), pipeline_mode=pl.Buffered(1)),
        pl.BlockSpec((None, None, D_MODEL, tn), lambda i, j: (layer, f, 0, j)),
        pl.BlockSpec((None, None, D_MODEL, tn), lambda i, j: (layer, f, 0, nj + j)),
    ]
    out_specs = [pl.BlockSpec((tm, tn), lambda i, j: (i, j))]
    out_shape = [jax.ShapeDtypeStruct((tok.rows, D_FF), BF)]
    operands = [hn, w1, w1]
    if cast_w2:
        assert D_FF % (ni * nj) == 0
        slab = D_FF // (ni * nj)
        in_specs.append(pl.BlockSpec((None, None, slab, D_MODEL), lambda i, j: (layer, f, i * nj + j, 0)))
        out_specs.append(pl.BlockSpec((slab, D_MODEL), lambda i, j: (i * nj + j, 0)))
        out_shape.append(jax.ShapeDtypeStruct((D_FF, D_MODEL), BF))
        operands.append(w2)
    return pl.pallas_call(
        functools.partial(_ffn_up_kernel, cast_w2=cast_w2),
        grid=(ni, nj),
        in_specs=in_specs,
        out_specs=out_specs,
        out_shape=out_shape,
        compiler_params=_params(2),
        name="ffn_up",
    )(*operands)


def _ffn(h, g, mt, layer, f, w1, w2, tok, w2_bf=None):
    chunk = 6 * f
    hn = _normmod(h, g, mt, layer, chunk, tok)
    if w2_bf is None:
        u, w2_bf = _ffn_up(hn, w1, w2, layer, f, tok, True)
    else:
        u, = _ffn_up(hn, w1, w2, layer, f, tok, False)
    tn = 512
    tile = pl.BlockSpec((tok.tm, tn), lambda i, j: (i, j))
    out, = _matmul(
        "ffn_down", u, [(w2_bf, (), 0)], [(h, tile), (mt, _mod_spec(tok, layer, chunk + 2, tn))],
        [((tok.rows, D_MODEL), F32, tile)], _epi_residual(0.5),
        rows=tok.rows, tm=tok.tm, tn=tn, n_cols=D_MODEL, lhs_single=True)
    return out, w2_bf


def _out_proj(name, o, w, prefix, h, mt, layer, tok):
    tn = 512
    tile = pl.BlockSpec((tok.tm, tn), lambda i, j: (i, j))
    out, = _matmul(
        name, o, [(w, prefix, 0)], [(h, tile), (mt, _mod_spec(tok, layer, 5, tn))],
        [((tok.rows, D_MODEL), F32, tile)], _epi_residual(1.0),
        rows=tok.rows, tm=tok.tm, tn=tn, n_cols=D_MODEL)
    return out


def _head_proj(name, u, w, prefix, col_off, width, kinds, gain, tabs, shift, tok, tn=512):
    out, = _matmul(
        name, u, [(w, prefix, col_off // tn)],
        [(gain, pl.BlockSpec((1, tn), lambda i, j: (0, j)))] + _rope_extras(tabs, tok),
        [((tok.rows, width), BF, pl.BlockSpec((tok.tm, tn), lambda i, j: (i, j)))],
        _epi_heads(kinds * (tn // (LANES * len(kinds))), shift),
        rows=tok.rows, tm=tok.tm, tn=tn, n_cols=width, lagged="NR" in kinds)
    return out


def _even_projections(u, w_in, j, gains, tok, want_q):
    na_q_g, na_k_g, gq_q_g, gq_k_g = gains
    scale = HEAD_DIM ** -0.5 * LOG2E
    rot = "NR" if tok.latent else "N"
    tabs = _rope128_tables() if tok.latent else []
    ones = jnp.ones((HEAD_DIM,), F32)
    proj = functools.partial(_head_proj, u=u, w=w_in, prefix=(j,), shift=_ROPE128_SHIFT, tok=tok)
    out = {}
    if want_q:
        out["na_q"] = proj("na_q", col_off=0, width=NA_WIDTH, kinds=("N",),
                           gain=_head_gain(na_q_g * scale, NA_WIDTH), tabs=[])
        out["gq_q"] = proj("gq_q", col_off=NA_WIDTH, width=GQA_Q_WIDTH, kinds=(rot,),
                           gain=_head_gain(gq_q_g * scale, GQA_Q_WIDTH), tabs=tabs)
    out["na_k"] = proj("na_k", col_off=EV_Q_COLS, width=NA_WIDTH, kinds=("N",),
                       gain=_head_gain(na_k_g, NA_WIDTH), tabs=[])
    out["na_v"] = proj("na_v", col_off=EV_Q_COLS + NA_WIDTH, width=NA_WIDTH, kinds=("P",),
                       gain=_head_gain(ones, NA_WIDTH), tabs=[])
    out["gq_k"] = proj("gq_k", col_off=EV_Q_COLS + 2 * NA_WIDTH, width=GQA_KV_WIDTH, kinds=(rot,),
                       gain=_head_gain(gq_k_g, GQA_KV_WIDTH), tabs=tabs)
    out["gq_v"] = proj("gq_v", col_off=EV_Q_COLS + 2 * NA_WIDTH + GQA_KV_WIDTH, width=GQA_KV_WIDTH,
                       kinds=("P",), gain=_head_gain(ones, GQA_KV_WIDTH), tabs=[])
    return out


def _even_mixer(ul, uc, hl, hc, mt, layer, w_in, w_out, j, gains, rpb, lat, ctx, ctx_out):
    pl_ = _even_projections(ul, w_in, j, gains, lat, True)
    pc = _even_projections(uc, w_in, j, gains, ctx, ctx_out)
    head = lambda h: h
    a = _na_attention(pl_["na_q"], pl_["na_k"], pl_["na_v"], pc["na_k"], pc["na_v"], _na_bias_tables(rpb))
    gq_segs = [(CTX_LEN, [(pc["gq_k"], head)], (pc["gq_v"], head)),
               (SEQ, [(pl_["gq_k"], head)], (pl_["gq_v"], head))]
    b = _attention("gqa", pl_["gq_q"], gq_segs, q_rows=SEQ, n_kv=GQA_KV_HEADS, group=GQA_GROUP,
                   dk=HEAD_DIM, dv=HEAD_DIM, tq=128)
    hl = _out_proj("ev_out", jnp.concatenate([a, b], axis=-1), w_out, (j,), hl, mt, layer, lat)
    if not ctx_out:
        return hl, hc
    ac = _attention("na_ctx", pc["na_q"], [(CTX_LEN, [(pc["na_k"], head)], (pc["na_v"], head))],
                    q_rows=CTX_LEN, n_kv=NA_HEADS, group=1, dk=HEAD_DIM, dv=HEAD_DIM, tq=CTX_LEN)
    bc = _attention("gqa_ctx", pc["gq_q"], gq_segs[:1], q_rows=CTX_LEN, n_kv=GQA_KV_HEADS, group=GQA_GROUP,
                    dk=HEAD_DIM, dv=HEAD_DIM, tq=128)
    hc = _out_proj("ev_out_ctx", jnp.concatenate([ac, bc], axis=-1), w_out, (j,), hc, mt, layer, ctx)
    return hl, hc


def _mla_down(u, w_down_p, gains, tok):
    q_a_g, kv_a_g, kr_g = gains
    tabs = _rope64_tables() if tok.latent else []
    tm = 512
    t = _Tokens(tok.rows, tm, tok.latent)
    full = lambda n: pl.BlockSpec((1, n), lambda i, j: (0, 0))
    rows = lambda n: pl.BlockSpec((tm, n), lambda i, j: (i, 0))
    return _matmul(
        "mla_down", u, [(w_down_p, (), 0)],
        [(q_a_g.reshape(1, -1), full(MLA_Q_RANK)), (kv_a_g.reshape(1, -1), full(MLA_KV_RANK)),
         (kr_g, full(LANES))] + _rope_extras(tabs, t),
        [((tok.rows, MLA_Q_RANK), BF, rows(MLA_Q_RANK)), ((tok.rows, MLA_KV_RANK), BF, rows(MLA_KV_RANK)),
         ((tok.rows, LANES), BF, rows(LANES))],
        _epi_mla_down(tok.latent), rows=tok.rows, tm=tm, tn=MLA_DOWN_COLS, n_cols=MLA_DOWN_COLS,
        groups=_MLA_DOWN_GROUPS)


def _rope_slot_gain(g, rotated):
    return jnp.concatenate([g, g if rotated else jnp.zeros_like(g)])


def _mla_mixer(ul, uc, hl, hc, mt, layer, w, g, lat, ctx, ctx_out):
    scale = MLA_QK ** -0.5 * LOG2E
    kv_gain = _head_gain(jnp.concatenate([g["kn"], jnp.ones((MLA_V,), F32)]), MLA_HEADS * (MLA_NOPE + MLA_V))
    tabs = _rope64_tables()

    def down(u, tok):
        kr_gain = _rope_slot_gain(g["kr"], tok.latent).reshape(1, LANES)
        return _mla_down(u, w["down"], (g["q_a"], g["kv_a"], kr_gain), tok)

    def keys_values(ckv, tok):
        return _head_proj("mla_kv", ckv, w["ukv"], (), 0, MLA_HEADS * (MLA_NOPE + MLA_V), ("N", "P"),
                          kv_gain, [], 0, tok)

    def queries(name, cq, tok):
        gain = jnp.concatenate([g["qn"], _rope_slot_gain(g["qr"], tok.latent)]) * scale
        return _head_proj(name, cq, w["uq"], (), 0, MLA_HEADS * MLA_Q_SLOT,
                          ("N", "NR" if tok.latent else "N"), _head_gain(gain, MLA_HEADS * MLA_Q_SLOT),
                          tabs if tok.latent else [], _ROPE64_SHIFT, tok)

    cq_l, ckv_l, kr_l = down(ul, lat)
    cq_c, ckv_c, kr_c = down(uc, ctx)
    kv_l = keys_values(ckv_l, lat)
    kv_c = keys_values(ckv_c, ctx)
    q_l = queries("mla_q", cq_l, lat)
    nope = lambda h: 2 * h
    val = lambda h: 2 * h + 1
    shared = lambda h: 0
    segs = [(CTX_LEN, [(kv_c, nope), (kr_c, shared)], (kv_c, val)),
            (SEQ, [(kv_l, nope), (kr_l, shared)], (kv_l, val))]
    o = _attention("mla", q_l, segs, q_rows=SEQ, n_kv=MLA_HEADS, group=1, dk=MLA_Q_SLOT, dv=MLA_V, tq=512)
    hl = _out_proj("mla_out", o, w["o"], (), hl, mt, layer, lat)
    if not ctx_out:
        return hl, hc
    q_c = queries("mla_q_ctx", cq_c, ctx)
    oc = _attention("mla_ctx", q_c, segs[:1], q_rows=CTX_LEN, n_kv=MLA_HEADS, group=1, dk=MLA_Q_SLOT,
                    dv=MLA_V, tq=CTX_LEN)
    hc = _out_proj("mla_out_ctx", oc, w["o"], (), hc, mt, layer, ctx)
    return hl, hc


def _mla_weights(w_down, w_uq, w_ukv, w_o, j):
    down = jnp.concatenate([w_down[j], w_down[j][:, -MLA_ROPE:]], axis=1).astype(BF)
    uq = w_uq[j].reshape(MLA_Q_RANK, MLA_HEADS, MLA_QK)
    uq = jnp.concatenate([uq, uq[:, :, -MLA_ROPE:]], axis=2).astype(BF)
    return {"down": down, "uq": uq.reshape(MLA_Q_RANK, MLA_HEADS * MLA_Q_SLOT), "ukv": w_ukv[j], "o": w_o[j]}


def kernel(x, c, ctx, c_ctx, norm_g, w_mod, b_mod, ffn_w1, ffn_w2, ev_w_in, ev_w_out, na_q_g, na_k_g, na_rpb, gq_q_g, gq_k_g, mla_w_down, mla_q_a_g, mla_kv_a_g, mla_w_uq, mla_w_ukv, mla_qn_g, mla_qr_g, mla_kn_g, mla_kr_g, mla_w_o):
    lat = _Tokens(BATCH * SEQ, 1024, True)
    ctk = _Tokens(BATCH * CTX_LEN, 512, False)
    cvec = jnp.zeros((MOD_ROWS, D_MODEL), F32).at[:BATCH].set(c).at[CTX_MOD_ROW].set(c_ctx)
    mt = _modulation(cvec, w_mod, b_mod).reshape(DEPTH, MOD_ROWS, 1, N_MOD * D_MODEL)
    hl = x.reshape(BATCH * SEQ, D_MODEL)
    hc = ctx.reshape(BATCH * CTX_LEN, D_MODEL)
    for i in range(DEPTH):
        last = i == DEPTH - 1
        g = norm_g[i]
        hl, w2_bf = _ffn(hl, g[0], mt, i, 0, ffn_w1, ffn_w2, lat)
        hc, _ = _ffn(hc, g[0], mt, i, 0, ffn_w1, ffn_w2, ctk, w2_bf)
        ul = _normmod(hl, g[1], mt, i, 3, lat)
        uc = _normmod(hc, g[1], mt, i, 3, ctk)
        j = i // 2
        if i % 2 == 0:
            gains = (na_q_g[j], na_k_g[j], gq_q_g[j], gq_k_g[j])
            hl, hc = _even_mixer(ul, uc, hl, hc, mt, i, ev_w_in, ev_w_out, j, gains, na_rpb[j], lat, ctk,
                                 not last)
        else:
            w = _mla_weights(mla_w_down, mla_w_uq, mla_w_ukv, mla_w_o, j)
            gains = {"q_a": mla_q_a_g[j], "kv_a": mla_kv_a_g[j], "qn": mla_qn_g[j], "qr": mla_qr_g[j],
                     "kn": mla_kn_g[j], "kr": mla_kr_g[j]}
            hl, hc = _mla_mixer(ul, uc, hl, hc, mt, i, w, gains, lat, ctk, not last)
        hl, w2_bf = _ffn(hl, g[2], mt, i, 1, ffn_w1, ffn_w2, lat)
        if not last:
            hc, _ = _ffn(hc, g[2], mt, i, 1, ffn_w1, ffn_w2, ctk, w2_bf)
    return hl.reshape(BATCH, SEQ, D_MODEL)
```

```python
import functools

import numpy as np
import jax
import jax.numpy as jnp
from jax import lax
from jax.experimental import pallas as pl
from jax.experimental.pallas import tpu as pltpu

D_MODEL = 4096
BATCH = 2
SEQ = 4096
DEPTH = 2
CTX_LEN = 256
GRID_W = 64
GRID_H = SEQ // GRID_W
HEAD_DIM = 128
EPS = 1e-6
ROPE_THETA = 10000.0
N_MOD = 9
D_FF = 10240
NA_HEADS = 16
NA_WIN_H = 8
NA_WIN_W = 16
GQA_Q_HEADS = 16
GQA_KV_HEADS = 4
GQA_GROUP = GQA_Q_HEADS // GQA_KV_HEADS
NA_WIDTH = NA_HEADS * HEAD_DIM
GQA_Q_WIDTH = GQA_Q_HEADS * HEAD_DIM
GQA_KV_WIDTH = GQA_KV_HEADS * HEAD_DIM
EV_Q_COLS = NA_WIDTH + GQA_Q_WIDTH
MLA_HEADS = 32
MLA_Q_RANK = 1024
MLA_KV_RANK = 512
MLA_NOPE = 128
MLA_ROPE = 64
MLA_V = 128
MLA_QK = MLA_NOPE + MLA_ROPE

LANES = 128
MLA_Q_SLOT = 2 * LANES
MLA_DOWN_COLS = MLA_Q_RANK + MLA_KV_RANK + LANES
MOD_ROWS = 8
CTX_MOD_ROW = BATCH

NA_GROUP_ROWS = 4
NA_BAND_ROWS = 12
NA_N_GROUPS = GRID_H // NA_GROUP_ROWS

ATTN_KEY_CHUNK = 256
LOG2E = 1.4426950408889634

VMEM_LIMIT = 56 * 1024 * 1024

BF = jnp.bfloat16
F32 = jnp.float32


def _params(n_axes):
    return pltpu.CompilerParams(dimension_semantics=("arbitrary",) * n_axes, vmem_limit_bytes=VMEM_LIMIT)


def _mod_kernel(c_ref, w_ref, b_ref, o_ref):
    c = c_ref[...]
    a = (c * jax.nn.sigmoid(c)).astype(BF)
    o_ref[...] = jnp.dot(a, w_ref[...].astype(BF), preferred_element_type=F32) + b_ref[...]


def _modulation(cvec, w_mod, b_mod):
    n = N_MOD * D_MODEL
    tn = 512
    return pl.pallas_call(
        _mod_kernel,
        grid=(DEPTH, n // tn),
        in_specs=[
            pl.BlockSpec((MOD_ROWS, D_MODEL), lambda l, j: (0, 0)),
            pl.BlockSpec((None, D_MODEL, tn), lambda l, j: (l, 0, j)),
            pl.BlockSpec((None, 1, tn), lambda l, j: (l, 0, j)),
        ],
        out_specs=pl.BlockSpec((None, MOD_ROWS, tn), lambda l, j: (l, 0, j)),
        out_shape=jax.ShapeDtypeStruct((DEPTH, MOD_ROWS, n), F32),
        compiler_params=_params(2),
        name="modulation",
    )(cvec, w_mod, b_mod.reshape(DEPTH, 1, n))


class _Tokens:
    def __init__(self, rows, tm, latent):
        self.rows, self.tm, self.latent = rows, tm, latent

    def mod_row(self, i):
        return i // (SEQ // self.tm) if self.latent else CTX_MOD_ROW

    def pos_block(self, i):
        return i % (SEQ // self.tm)


def _mod_spec(tok, layer, chunk, tn):
    return pl.BlockSpec((None, None, 1, tn),
                        lambda i, j: (layer, tok.mod_row(i), 0, chunk * (D_MODEL // tn) + j))


def _normmod_kernel(x_ref, g_ref, sh_ref, sc_ref, o_ref):
    x = x_ref[...]
    ms = jnp.mean(x * x, axis=-1, keepdims=True)
    y = x * lax.rsqrt(ms + EPS) * g_ref[...]
    o_ref[...] = (y * (1.0 + sc_ref[...]) + sh_ref[...]).astype(o_ref.dtype)


def _normmod(x, g, mt, layer, chunk, tok):
    tm = 256
    t = _Tokens(tok.rows, tm, tok.latent)
    sh = _mod_spec(t, layer, chunk, D_MODEL)
    sc = _mod_spec(t, layer, chunk + 1, D_MODEL)
    return pl.pallas_call(
        _normmod_kernel,
        grid=(tok.rows // tm,),
        in_specs=[
            pl.BlockSpec((tm, D_MODEL), lambda i: (i, 0)),
            pl.BlockSpec((1, D_MODEL), lambda i: (0, 0)),
            pl.BlockSpec(sh.block_shape, lambda i: sh.index_map(i, 0)),
            pl.BlockSpec(sc.block_shape, lambda i: sc.index_map(i, 0)),
        ],
        out_specs=pl.BlockSpec((tm, D_MODEL), lambda i: (i, 0)),
        out_shape=jax.ShapeDtypeStruct((tok.rows, D_MODEL), BF),
        compiler_params=_params(1),
        name="normmod",
    )(x, g.reshape(1, D_MODEL), mt, mt)


def _mm_kernel(*refs, n_rhs, n_extra, n_out, groups, epilogue):
    lhs_ref = refs[0]
    rhs_refs = refs[1:1 + n_rhs]
    extra = refs[1 + n_rhs:1 + n_rhs + n_extra]
    out_refs = refs[1 + n_rhs + n_extra:]
    acc_refs = out_refs[n_out:]
    out_refs = out_refs[:n_out]
    a = lhs_ref[...]
    if not acc_refs:
        for c0, c1 in groups:
            accs = [jnp.dot(a, r[:, c0:c1].astype(BF), preferred_element_type=F32) for r in rhs_refs]
            epilogue(c0, c1, accs, extra, out_refs)
        return

    @pl.when(pl.program_id(0) == 0)
    def _():
        for acc in acc_refs:
            acc[...] = jnp.zeros_like(acc)

    for c0, c1 in groups:
        epilogue(c0, c1, [acc[:, c0:c1] for acc in acc_refs], extra, out_refs)
    for c0, c1 in groups:
        for r, acc in zip(rhs_refs, acc_refs):
            acc[:, c0:c1] = jnp.dot(a, r[:, c0:c1].astype(BF), preferred_element_type=F32)


MXU_COLS = 256


def _matmul(name, lhs, rhs, extras, outs, epilogue, *, rows, tm, tn, n_cols, lhs_single=False, groups=None,
            lagged=False):
    if groups is None:
        groups = tuple((c, c + MXU_COLS) for c in range(0, tn, MXU_COLS))
    kdim = lhs.shape[1]
    ni, nj = rows // tm, n_cols // tn
    if lagged:
        n_tiles = ni * nj
        grid = (n_tiles + 1,)
        cur = lambda f: (lambda t: f(*divmod(jnp.minimum(t, n_tiles - 1), nj)))
        prev = lambda f: (lambda t: f(*divmod(jnp.maximum(t - 1, 0), nj)))
        scratch = [pltpu.VMEM((tm, tn), F32) for _ in rhs]
    else:
        grid = (ni, nj)
        cur = prev = lambda f: f
        scratch = []
    lhs_mode = pl.Buffered(1) if lhs_single else None
    in_specs = [pl.BlockSpec((tm, kdim), cur(lambda i, j: (i, 0)), pipeline_mode=lhs_mode)]
    for _, prefix, off in rhs:
        in_specs.append(pl.BlockSpec((None,) * len(prefix) + (kdim, tn),
                                     cur(lambda i, j, prefix=prefix, off=off: prefix + (0, off + j))))
    in_specs += [pl.BlockSpec(spec.block_shape, prev(spec.index_map)) for _, spec in extras]
    kern = functools.partial(_mm_kernel, n_rhs=len(rhs), n_extra=len(extras), n_out=len(outs),
                             groups=groups, epilogue=epilogue)
    return pl.pallas_call(
        kern,
        grid=grid,
        in_specs=in_specs,
        out_specs=[pl.BlockSpec(spec.block_shape, prev(spec.index_map)) for _, _, spec in outs],
        out_shape=[jax.ShapeDtypeStruct(shape, dtype) for shape, dtype, _ in outs],
        scratch_shapes=scratch,
        compiler_params=_params(len(grid)),
        name=name,
    )(lhs, *[r[0] for r in rhs], *[e[0] for e in extras])


def _epi_residual(coef):
    def epi(c0, c1, accs, extra, outs):
        res_ref, gate_ref = extra
        outs[0][:, c0:c1] = res_ref[:, c0:c1] + (coef * gate_ref[:, c0:c1]) * accs[0]
    return epi


def _rms_chunk(x, gain):
    ms = jnp.mean(x * x, axis=-1, keepdims=True)
    return x * lax.rsqrt(ms + EPS) * gain


def _rotate(x, tabs, shift):
    return x * tabs[0][...] + pltpu.roll(x, shift, 1) * tabs[1][...]


def _epi_heads(kinds, shift):
    def epi(c0, c1, accs, extra, outs):
        gain_ref, tabs = extra[0], extra[1:]
        for c in range(c0, c1, LANES):
            kind = kinds[c // LANES]
            x = accs[0][:, c - c0:c - c0 + LANES]
            if kind != "P":
                x = _rms_chunk(x, gain_ref[:, c:c + LANES])
            if kind == "NR":
                x = _rotate(x, tabs, shift)
            outs[0][:, c:c + LANES] = x.astype(outs[0].dtype)
    return epi


_MLA_DOWN_GROUPS = ((0, MLA_Q_RANK), (MLA_Q_RANK, MLA_Q_RANK + MLA_KV_RANK),
                    (MLA_Q_RANK + MLA_KV_RANK, MLA_DOWN_COLS))


def _epi_mla_down(rope):
    def epi(c0, c1, accs, extra, outs):
        which = [g[0] for g in _MLA_DOWN_GROUPS].index(c0)
        x = _rms_chunk(accs[0], extra[which][...])
        if rope and which == 2:
            x = _rotate(x, extra[3:], _ROPE64_SHIFT)
        outs[which][...] = x.astype(outs[which].dtype)
    return epi


def _axial_angles(rot_dim):
    t = jnp.arange(SEQ)
    row = (t // GRID_W).astype(F32)
    col = (t % GRID_W).astype(F32)
    axis_dim = rot_dim // 2
    inv_freq = ROPE_THETA ** (-jnp.arange(0, axis_dim, 2, dtype=F32) / axis_dim)
    ang = jnp.concatenate([row[:, None] * inv_freq, col[:, None] * inv_freq], axis=-1)
    return jnp.cos(ang), jnp.sin(ang)


_ROPE128_SHIFT = HEAD_DIM // 2
_ROPE64_SHIFT = LANES - MLA_ROPE // 2


def _rope128_tables():
    cos, sin = _axial_angles(HEAD_DIM)
    return [jnp.concatenate([cos, cos], -1), jnp.concatenate([-sin, sin], -1)]


def _rope64_tables():
    cos, sin = _axial_angles(MLA_ROPE)
    z64 = jnp.zeros((SEQ, LANES - MLA_ROPE), F32)
    return [jnp.concatenate([cos, cos, z64], -1), jnp.concatenate([-sin, sin, z64], -1)]


def _rope_extras(tabs, tok):
    return [(t, pl.BlockSpec((tok.tm, LANES), lambda i, j: (tok.pos_block(i), 0))) for t in tabs]


def _attn_kernel(*refs, n_seg, n_parts, group, dk, dv, tq, seg_lens, chunk):
    q_ref = refs[0]
    k_refs = refs[1:1 + n_seg * n_parts]
    v_refs = refs[1 + n_seg * n_parts:1 + n_seg * n_parts + n_seg]
    o_ref, k_scr, v_scr = refs[1 + n_seg * n_parts + n_seg:]

    @pl.when(pl.program_id(2) == 0)
    def _():
        off = 0
        for s in range(n_seg):
            length = seg_lens[s]
            col = 0
            for part in k_refs[s * n_parts:(s + 1) * n_parts]:
                width = part.shape[-1]
                k_scr[off:off + length, col:col + width] = part[...]
                col += width
            v_scr[off:off + length, :] = v_refs[s][...]
            off += length

    q = q_ref[...]
    if group > 1:
        q = jnp.concatenate([q[:, g * dk:(g + 1) * dk] for g in range(group)], axis=0)
    total = sum(seg_lens)
    m = l = acc = None
    for c0 in range(0, total, chunk):
        c1 = min(c0 + chunk, total)
        s = lax.dot_general(q, k_scr[c0:c1, :], (((1,), (1,)), ((), ())), preferred_element_type=F32)
        mc = jnp.max(s, axis=-1, keepdims=True)
        if m is None:
            m_new = mc
        else:
            m_new = jnp.maximum(m, mc)
            alpha = jnp.exp2(m - m_new)
        p = jnp.exp2(s - m_new)
        pv = jnp.dot(p.astype(BF), v_scr[c0:c1, :], preferred_element_type=F32)
        ps = jnp.sum(p, axis=-1, keepdims=True)
        if m is None:
            l, acc = ps, pv
        else:
            l, acc = alpha * l + ps, alpha * acc + pv
        m = m_new
    o = acc / l
    for g in range(group):
        o_ref[:, g * dv:(g + 1) * dv] = o[g * tq:(g + 1) * tq].astype(o_ref.dtype)


def _attention(name, q, segs, *, q_rows, n_kv, group, dk, dv, tq):
    nq = q_rows // tq
    n_parts = len(segs[0][1])
    in_specs = [pl.BlockSpec((tq, group * dk), lambda b, h, i: (b * nq + i, h))]
    operands = [q]
    for length, k_parts, _ in segs:
        for arr, col in k_parts:
            in_specs.append(pl.BlockSpec((length, LANES), lambda b, h, i, col=col: (b, col(h))))
            operands.append(arr)
    for length, _, (arr, col) in segs:
        in_specs.append(pl.BlockSpec((length, dv), lambda b, h, i, col=col: (b, col(h))))
        operands.append(arr)
    total = sum(s[0] for s in segs)
    kern = functools.partial(_attn_kernel, n_seg=len(segs), n_parts=n_parts, group=group, dk=dk, dv=dv,
                             tq=tq, seg_lens=tuple(s[0] for s in segs), chunk=ATTN_KEY_CHUNK)
    return pl.pallas_call(
        kern,
        grid=(BATCH, n_kv, nq),
        in_specs=in_specs,
        out_specs=pl.BlockSpec((tq, group * dv), lambda b, h, i: (b * nq + i, h)),
        out_shape=jax.ShapeDtypeStruct((BATCH * q_rows, n_kv * group * dv), BF),
        scratch_shapes=[pltpu.VMEM((total, dk), BF), pltpu.VMEM((total, dv), BF)],
        compiler_params=_params(3),
        name=name,
    )(*operands)


def _na_band_start(g):
    return min(max(NA_GROUP_ROWS * g - NA_WIN_H // 2, 0), GRID_H - NA_BAND_ROWS)


def _na_variant(g):
    return min(g, 1) + (g == NA_N_GROUPS - 1)


def _na_kernel(q_ref, k_ref, v_ref, kc_ref, vc_ref, bias_ref, o_ref):
    tq = NA_GROUP_ROWS * GRID_W
    band = NA_BAND_ROWS * GRID_W
    nt = (((1,), (1,)), ((), ()))
    kc = kc_ref[...]
    vc = vc_ref[...]
    for g in range(NA_N_GROUPS):
        start = _na_band_start(g) * GRID_W
        q = q_ref[g * tq:(g + 1) * tq, :]
        kb = k_ref[start:start + band, :]
        vb = v_ref[start:start + band, :]
        s_b = lax.dot_general(q, kb, nt, preferred_element_type=F32) + bias_ref[_na_variant(g)]
        s_c = lax.dot_general(q, kc, nt, preferred_element_type=F32)
        m = jnp.maximum(jnp.max(s_b, axis=-1, keepdims=True), jnp.max(s_c, axis=-1, keepdims=True))
        p_b = jnp.exp2(s_b - m)
        p_c = jnp.exp2(s_c - m)
        l = jnp.sum(p_b, axis=-1, keepdims=True) + jnp.sum(p_c, axis=-1, keepdims=True)
        o = (jnp.dot(p_b.astype(BF), vb, preferred_element_type=F32)
             + jnp.dot(p_c.astype(BF), vc, preferred_element_type=F32))
        o_ref[g * tq:(g + 1) * tq, :] = (o / l).astype(o_ref.dtype)


def _na_band_rows():
    ri = np.arange(NA_GROUP_ROWS)
    bi = np.arange(NA_BAND_ROWS)
    row_idx, row_ok = [], []
    for g in (0, 1, NA_N_GROUPS - 1):
        r = NA_GROUP_ROWS * g + ri
        r0 = np.clip(r - NA_WIN_H // 2, 0, GRID_H - NA_WIN_H)
        kr = _na_band_start(g) + bi
        row_ok.append((kr[None, :] >= r0[:, None]) & (kr[None, :] < r0[:, None] + NA_WIN_H))
        row_idx.append(np.clip(kr[None, :] - r[:, None] + NA_WIN_H - 1, 0, 2 * NA_WIN_H - 2))
    return np.stack(row_idx), np.stack(row_ok)


def _na_bias_kernel(f_ref, o_ref):
    _, row_ok = _na_band_rows()
    lane = lax.broadcasted_iota(jnp.int32, (GRID_W, LANES), 1)
    qc = lax.broadcasted_iota(jnp.int32, (GRID_W, LANES), 0)
    kc = lane % GRID_W
    left = lane < GRID_W
    c0 = jnp.clip(qc - NA_WIN_W // 2, 0, GRID_W - NA_WIN_W)
    col_ok = (kc >= c0) & (kc < c0 + NA_WIN_W)
    for v in range(3):
        for ri in range(NA_GROUP_ROWS):
            for bp in range(NA_BAND_ROWS // 2):
                halves = []
                for half in range(2):
                    row = (v * NA_GROUP_ROWS + ri) * NA_BAND_ROWS + 2 * bp + half
                    fb = jnp.broadcast_to(f_ref[row:row + 1, :], (GRID_W, LANES))
                    shift = (GRID_W * half - (GRID_W - 1)) % LANES
                    halves.append(pltpu.roll(fb, shift, 1, stride=1, stride_axis=0))
                ok_l, ok_r = bool(row_ok[v, ri, 2 * bp]), bool(row_ok[v, ri, 2 * bp + 1])
                ok = col_ok & ((left & ok_l) | (~left & ok_r))
                tile = jnp.where(left, halves[0], halves[1]) * LOG2E
                o_ref[v, ri * GRID_W:(ri + 1) * GRID_W, bp * LANES:(bp + 1) * LANES] = (
                    jnp.where(ok, tile, -jnp.inf))


def _na_bias_tables(rpb):
    row_idx, _ = _na_band_rows()
    n_rows = row_idx.size
    by_row = jnp.take(rpb, row_idx.reshape(-1), axis=1)
    pad = GRID_W - NA_WIN_W
    f = jnp.pad(by_row, ((0, 0), (0, 0), (pad, LANES - pad - by_row.shape[-1])))
    shape = (3, NA_GROUP_ROWS * GRID_W, NA_BAND_ROWS * GRID_W)
    return pl.pallas_call(
        _na_bias_kernel,
        grid=(NA_HEADS,),
        in_specs=[pl.BlockSpec((None, n_rows, LANES), lambda h: (h, 0, 0))],
        out_specs=pl.BlockSpec((None,) + shape, lambda h: (h, 0, 0, 0)),
        out_shape=jax.ShapeDtypeStruct((NA_HEADS,) + shape, F32),
        compiler_params=_params(1),
        name="na_bias",
    )(f)


def _na_attention(q, k, v, kc, vc, bias):
    head_block = lambda b, h: (b, h)
    return pl.pallas_call(
        _na_kernel,
        grid=(BATCH, NA_HEADS),
        in_specs=[
            pl.BlockSpec((SEQ, HEAD_DIM), head_block),
            pl.BlockSpec((SEQ, HEAD_DIM), head_block),
            pl.BlockSpec((SEQ, HEAD_DIM), head_block),
            pl.BlockSpec((CTX_LEN, HEAD_DIM), head_block),
            pl.BlockSpec((CTX_LEN, HEAD_DIM), head_block),
            pl.BlockSpec((None,) + bias.shape[1:], lambda b, h: (h, 0, 0, 0)),
        ],
        out_specs=pl.BlockSpec((SEQ, HEAD_DIM), head_block),
        out_shape=jax.ShapeDtypeStruct((BATCH * SEQ, NA_WIDTH), BF),
        compiler_params=_params(2),
        name="na_attention",
    )(q, k, v, kc, vc, bias)


def _head_gain(g, width):
    return jnp.tile(g, width // g.shape[0]).reshape(1, width)


def _ffn_up_kernel(*refs, cast_w2):
    if cast_w2:
        x_ref, wg_ref, wu_ref, w2_ref, o_ref, w2_bf_ref = refs
        w2_bf_ref[...] = w2_ref[...].astype(BF)
    else:
        x_ref, wg_ref, wu_ref, o_ref = refs
    x = x_ref[...]
    gate = jnp.dot(x, wg_ref[...].astype(BF), preferred_element_type=F32)
    up = jnp.dot(x, wu_ref[...].astype(BF), preferred_element_type=F32)
    o_ref[...] = (gate * jax.nn.sigmoid(gate) * up).astype(o_ref.dtype)


def _ffn_up(hn, w1, w2, layer, f, tok, cast_w2):
    tm = 2048 if tok.latent else tok.rows
    tn = 256
    ni, nj = tok.rows // tm, D_FF // tn
    in_specs = [
        pl.BlockSpec((tm, D_MODEL), lambda i, j: (i, 0), pipeline_mode=pl.Buffered(1)),
        pl.BlockSpec((None, None, D_MODEL, tn), lambda i, j: (layer, f, 0, j)),
        pl.BlockSpec((None, None, D_MODEL, tn), lambda i, j: (layer, f, 0, nj + j)),
    ]
    out_specs = [pl.BlockSpec((tm, tn), lambda i, j: (i, j))]
    out_shape = [jax.ShapeDtypeStruct((tok.rows, D_FF), BF)]
    operands = [hn, w1, w1]
    if cast_w2:
        assert D_FF % (ni * nj) == 0
        slab = D_FF // (ni * nj)
        in_specs.append(pl.BlockSpec((None, None, slab, D_MODEL), lambda i, j: (layer, f, i * nj + j, 0)))
        out_specs.append(pl.BlockSpec((slab, D_MODEL), lambda i, j: (i * nj + j, 0)))
        out_shape.append(jax.ShapeDtypeStruct((D_FF, D_MODEL), BF))
        operands.append(w2)
    return pl.pallas_call(
        functools.partial(_ffn_up_kernel, cast_w2=cast_w2),
        grid=(ni, nj),
        in_specs=in_specs,
        out_specs=out_specs,
        out_shape=out_shape,
        compiler_params=_params(2),
        name="ffn_up",
    )(*operands)


def _ffn(h, g, mt, layer, f, w1, w2, tok, w2_bf=None):
    chunk = 6 * f
    hn = _normmod(h, g, mt, layer, chunk, tok)
    if w2_bf is None:
        u, w2_bf = _ffn_up(hn, w1, w2, layer, f, tok, True)
    else:
        u, = _ffn_up(hn, w1, w2, layer, f, tok, False)
    tn = 512
    tile = pl.BlockSpec((tok.tm, tn), lambda i, j: (i, j))
    out, = _matmul(
        "ffn_down", u, [(w2_bf, (), 0)], [(h, tile), (mt, _mod_spec(tok, layer, chunk + 2, tn))],
        [((tok.rows, D_MODEL), F32, tile)], _epi_residual(0.5),
        rows=tok.rows, tm=tok.tm, tn=tn, n_cols=D_MODEL, lhs_single=True)
    return out, w2_bf


def _out_proj(name, o, w, prefix, h, mt, layer, tok):
    tn = 512
    tile = pl.BlockSpec((tok.tm, tn), lambda i, j: (i, j))
    out, = _matmul(
        name, o, [(w, prefix, 0)], [(h, tile), (mt, _mod_spec(tok, layer, 5, tn))],
        [((tok.rows, D_MODEL), F32, tile)], _epi_residual(1.0),
        rows=tok.rows, tm=tok.tm, tn=tn, n_cols=D_MODEL)
    return out


def _head_proj(name, u, w, prefix, col_off, width, kinds, gain, tabs, shift, tok, tn=512):
    out, = _matmul(
        name, u, [(w, prefix, col_off // tn)],
        [(gain, pl.BlockSpec((1, tn), lambda i, j: (0, j)))] + _rope_extras(tabs, tok),
        [((tok.rows, width), BF, pl.BlockSpec((tok.tm, tn), lambda i, j: (i, j)))],
        _epi_heads(kinds * (tn // (LANES * len(kinds))), shift),
        rows=tok.rows, tm=tok.tm, tn=tn, n_cols=width, lagged="NR" in kinds)
    return out


def _even_projections(u, w_in, j, gains, tok, want_q):
    na_q_g, na_k_g, gq_q_g, gq_k_g = gains
    scale = HEAD_DIM ** -0.5 * LOG2E
    rot = "NR" if tok.latent else "N"
    tabs = _rope128_tables() if tok.latent else []
    ones = jnp.ones((HEAD_DIM,), F32)
    proj = functools.partial(_head_proj, u=u, w=w_in, prefix=(j,), shift=_ROPE128_SHIFT, tok=tok)
    out = {}
    if want_q:
        out["na_q"] = proj("na_q", col_off=0, width=NA_WIDTH, kinds=("N",),
                           gain=_head_gain(na_q_g * scale, NA_WIDTH), tabs=[])
        out["gq_q"] = proj("gq_q", col_off=NA_WIDTH, width=GQA_Q_WIDTH, kinds=(rot,),
                           gain=_head_gain(gq_q_g * scale, GQA_Q_WIDTH), tabs=tabs)
    out["na_k"] = proj("na_k", col_off=EV_Q_COLS, width=NA_WIDTH, kinds=("N",),
                       gain=_head_gain(na_k_g, NA_WIDTH), tabs=[])
    out["na_v"] = proj("na_v", col_off=EV_Q_COLS + NA_WIDTH, width=NA_WIDTH, kinds=("P",),
                       gain=_head_gain(ones, NA_WIDTH), tabs=[])
    out["gq_k"] = proj("gq_k", col_off=EV_Q_COLS + 2 * NA_WIDTH, width=GQA_KV_WIDTH, kinds=(rot,),
                       gain=_head_gain(gq_k_g, GQA_KV_WIDTH), tabs=tabs)
    out["gq_v"] = proj("gq_v", col_off=EV_Q_COLS + 2 * NA_WIDTH + GQA_KV_WIDTH, width=GQA_KV_WIDTH,
                       kinds=("P",), gain=_head_gain(ones, GQA_KV_WIDTH), tabs=[])
    return out


def _even_mixer(ul, uc, hl, hc, mt, layer, w_in, w_out, j, gains, rpb, lat, ctx, ctx_out):
    pl_ = _even_projections(ul, w_in, j, gains, lat, True)
    pc = _even_projections(uc, w_in, j, gains, ctx, ctx_out)
    head = lambda h: h
    a = _na_attention(pl_["na_q"], pl_["na_k"], pl_["na_v"], pc["na_k"], pc["na_v"], _na_bias_tables(rpb))
    gq_segs = [(CTX_LEN, [(pc["gq_k"], head)], (pc["gq_v"], head)),
               (SEQ, [(pl_["gq_k"], head)], (pl_["gq_v"], head))]
    b = _attention("gqa", pl_["gq_q"], gq_segs, q_rows=SEQ, n_kv=GQA_KV_HEADS, group=GQA_GROUP,
                   dk=HEAD_DIM, dv=HEAD_DIM, tq=1024)
    hl = _out_proj("ev_out", jnp.concatenate([a, b], axis=-1), w_out, (j,), hl, mt, layer, lat)
    if not ctx_out:
        return hl, hc
    ac = _attention("na_ctx", pc["na_q"], [(CTX_LEN, [(pc["na_k"], head)], (pc["na_v"], head))],
                    q_rows=CTX_LEN, n_kv=NA_HEADS, group=1, dk=HEAD_DIM, dv=HEAD_DIM, tq=CTX_LEN)
    bc = _attention("gqa_ctx", pc["gq_q"], gq_segs[:1], q_rows=CTX_LEN, n_kv=GQA_KV_HEADS, group=GQA_GROUP,
                    dk=HEAD_DIM, dv=HEAD_DIM, tq=128)
    hc = _out_proj("ev_out_ctx", jnp.concatenate([ac, bc], axis=-1), w_out, (j,), hc, mt, layer, ctx)
    return hl, hc


def _mla_down(u, w_down_p, gains, tok):
    q_a_g, kv_a_g, kr_g = gains
    tabs = _rope64_tables() if tok.latent else []
    tm = 512
    t = _Tokens(tok.rows, tm, tok.latent)
    full = lambda n: pl.BlockSpec((1, n), lambda i, j: (0, 0))
    rows = lambda n: pl.BlockSpec((tm, n), lambda i, j: (i, 0))
    return _matmul(
        "mla_down", u, [(w_down_p, (), 0)],
        [(q_a_g.reshape(1, -1), full(MLA_Q_RANK)), (kv_a_g.reshape(1, -1), full(MLA_KV_RANK)),
         (kr_g, full(LANES))] + _rope_extras(tabs, t),
        [((tok.rows, MLA_Q_RANK), BF, rows(MLA_Q_RANK)), ((tok.rows, MLA_KV_RANK), BF, rows(MLA_KV_RANK)),
         ((tok.rows, LANES), BF, rows(LANES))],
        _epi_mla_down(tok.latent), rows=tok.rows, tm=tm, tn=MLA_DOWN_COLS, n_cols=MLA_DOWN_COLS,
        groups=_MLA_DOWN_GROUPS)


def _rope_slot_gain(g, rotated):
    return jnp.concatenate([g, g if rotated else jnp.zeros_like(g)])


def _mla_mixer(ul, uc, hl, hc, mt, layer, w, g, lat, ctx, ctx_out):
    scale = MLA_QK ** -0.5 * LOG2E
    kv_gain = _head_gain(jnp.concatenate([g["kn"], jnp.ones((MLA_V,), F32)]), MLA_HEADS * (MLA_NOPE + MLA_V))
    tabs = _rope64_tables()

    def down(u, tok):
        kr_gain = _rope_slot_gain(g["kr"], tok.latent).reshape(1, LANES)
        return _mla_down(u, w["down"], (g["q_a"], g["kv_a"], kr_gain), tok)

    def keys_values(ckv, tok):
        return _head_proj("mla_kv", ckv, w["ukv"], (), 0, MLA_HEADS * (MLA_NOPE + MLA_V), ("N", "P"),
                          kv_gain, [], 0, tok)

    def queries(name, cq, tok):
        gain = jnp.concatenate([g["qn"], _rope_slot_gain(g["qr"], tok.latent)]) * scale
        return _head_proj(name, cq, w["uq"], (), 0, MLA_HEADS * MLA_Q_SLOT,
                          ("N", "NR" if tok.latent else "N"), _head_gain(gain, MLA_HEADS * MLA_Q_SLOT),
                          tabs if tok.latent else [], _ROPE64_SHIFT, tok)

    cq_l, ckv_l, kr_l = down(ul, lat)
    cq_c, ckv_c, kr_c = down(uc, ctx)
    kv_l = keys_values(ckv_l, lat)
    kv_c = keys_values(ckv_c, ctx)
    q_l = queries("mla_q", cq_l, lat)
    nope = lambda h: 2 * h
    val = lambda h: 2 * h + 1
    shared = lambda h: 0
    segs = [(CTX_LEN, [(kv_c, nope), (kr_c, shared)], (kv_c, val)),
            (SEQ, [(kv_l, nope), (kr_l, shared)], (kv_l, val))]
    o = _attention("mla", q_l, segs, q_rows=SEQ, n_kv=MLA_HEADS, group=1, dk=MLA_Q_SLOT, dv=MLA_V, tq=4096)
    hl = _out_proj("mla_out", o, w["o"], (), hl, mt, layer, lat)
    if not ctx_out:
        return hl, hc
    q_c = queries("mla_q_ctx", cq_c, ctx)
    oc = _attention("mla_ctx", q_c, segs[:1], q_rows=CTX_LEN, n_kv=MLA_HEADS, group=1, dk=MLA_Q_SLOT,
                    dv=MLA_V, tq=CTX_LEN)
    hc = _out_proj("mla_out_ctx", oc, w["o"], (), hc, mt, layer, ctx)
    return hl, hc


def _mla_weights(w_down, w_uq, w_ukv, w_o, j):
    down = jnp.concatenate([w_down[j], w_down[j][:, -MLA_ROPE:]], axis=1).astype(BF)
    uq = w_uq[j].reshape(MLA_Q_RANK, MLA_HEADS, MLA_QK)
    uq = jnp.concatenate([uq, uq[:, :, -MLA_ROPE:]], axis=2).astype(BF)
    return {"down": down, "uq": uq.reshape(MLA_Q_RANK, MLA_HEADS * MLA_Q_SLOT), "ukv": w_ukv[j], "o": w_o[j]}


def kernel(x, c, ctx, c_ctx, norm_g, w_mod, b_mod, ffn_w1, ffn_w2, ev_w_in, ev_w_out, na_q_g, na_k_g, na_rpb, gq_q_g, gq_k_g, mla_w_down, mla_q_a_g, mla_kv_a_g, mla_w_uq, mla_w_ukv, mla_qn_g, mla_qr_g, mla_kn_g, mla_kr_g, mla_w_o):
    lat = _Tokens(BATCH * SEQ, 1024, True)
    ctk = _Tokens(BATCH * CTX_LEN, 512, False)
    cvec = jnp.zeros((MOD_ROWS, D_MODEL), F32).at[:BATCH].set(c).at[CTX_MOD_ROW].set(c_ctx)
    mt = _modulation(cvec, w_mod, b_mod).reshape(DEPTH, MOD_ROWS, 1, N_MOD * D_MODEL)
    hl = x.reshape(BATCH * SEQ, D_MODEL)
    hc = ctx.reshape(BATCH * CTX_LEN, D_MODEL)
    for i in range(DEPTH):
        last = i == DEPTH - 1
        g = norm_g[i]
        hl, w2_bf = _ffn(hl, g[0], mt, i, 0, ffn_w1, ffn_w2, lat)
        hc, _ = _ffn(hc, g[0], mt, i, 0, ffn_w1, ffn_w2, ctk, w2_bf)
        ul = _normmod(hl, g[1], mt, i, 3, lat)
        uc = _normmod(hc, g[1], mt, i, 3, ctk)
        j = i // 2
        if i % 2 == 0:
            gains = (na_q_g[j], na_k_g[j], gq_q_g[j], gq_k_g[j])
            hl, hc = _even_mixer(ul, uc, hl, hc, mt, i, ev_w_in, ev_w_out, j, gains, na_rpb[j], lat, ctk,
                                 not last)
        else:
            w = _mla_weights(mla_w_down, mla_w_uq, mla_w_ukv, mla_w_o, j)
            gains = {"q_a": mla_q_a_g[j], "kv_a": mla_kv_a_g[j], "qn": mla_qn_g[j], "qr": mla_qr_g[j],
                     "kn": mla_kn_g[j], "kr": mla_kr_g[j]}
            hl, hc = _mla_mixer(ul, uc, hl, hc, mt, i, w, gains, lat, ctk, not last)
        hl, w2_bf = _ffn(hl, g[2], mt, i, 1, ffn_w1, ffn_w2, lat)
        if not last:
            hc, _ = _ffn(hc, g[2], mt, i, 1, ffn_w1, ffn_w2, ctk, w2_bf)
    return hl.reshape(BATCH, SEQ, D_MODEL)
```

```python
import functools

import numpy as np
import jax
import jax.numpy as jnp
from jax import lax
from jax.experimental import pallas as pl
from jax.experimental.pallas import tpu as pltpu

D_MODEL = 4096
BATCH = 2
SEQ = 4096
DEPTH = 2
CTX_LEN = 256
GRID_W = 64
GRID_H = SEQ // GRID_W
HEAD_DIM = 128
EPS = 1e-6
ROPE_THETA = 10000.0
N_MOD = 9
D_FF = 10240
NA_HEADS = 16
NA_WIN_H = 8
NA_WIN_W = 16
GQA_Q_HEADS = 16
GQA_KV_HEADS = 4
GQA_GROUP = GQA_Q_HEADS // GQA_KV_HEADS
NA_WIDTH = NA_HEADS * HEAD_DIM
GQA_Q_WIDTH = GQA_Q_HEADS * HEAD_DIM
GQA_KV_WIDTH = GQA_KV_HEADS * HEAD_DIM
EV_Q_COLS = NA_WIDTH + GQA_Q_WIDTH
MLA_HEADS = 32
MLA_Q_RANK = 1024
MLA_KV_RANK = 512
MLA_NOPE = 128
MLA_ROPE = 64
MLA_V = 128
MLA_QK = MLA_NOPE + MLA_ROPE

LANES = 128
MLA_Q_SLOT = 2 * LANES
MLA_DOWN_COLS = MLA_Q_RANK + MLA_KV_RANK + LANES
MOD_ROWS = 8
CTX_MOD_ROW = BATCH

NA_GROUP_ROWS = 4
NA_BAND_ROWS = 12
NA_N_GROUPS = GRID_H // NA_GROUP_ROWS

ATTN_KEY_CHUNK = 256
LOG2E = 1.4426950408889634

VMEM_LIMIT = 56 * 1024 * 1024

BF = jnp.bfloat16
F32 = jnp.float32


def _params(n_axes):
    return pltpu.CompilerParams(dimension_semantics=("arbitrary",) * n_axes, vmem_limit_bytes=VMEM_LIMIT)


def _mod_kernel(c_ref, w_ref, b_ref, o_ref):
    c = c_ref[...]
    a = (c * jax.nn.sigmoid(c)).astype(BF)
    o_ref[...] = jnp.dot(a, w_ref[...].astype(BF), preferred_element_type=F32) + b_ref[...]


def _modulation(cvec, w_mod, b_mod):
    n = N_MOD * D_MODEL
    tn = 512
    return pl.pallas_call(
        _mod_kernel,
        grid=(DEPTH, n // tn),
        in_specs=[
            pl.BlockSpec((MOD_ROWS, D_MODEL), lambda l, j: (0, 0)),
            pl.BlockSpec((None, D_MODEL, tn), lambda l, j: (l, 0, j)),
            pl.BlockSpec((None, 1, tn), lambda l, j: (l, 0, j)),
        ],
        out_specs=pl.BlockSpec((None, MOD_ROWS, tn), lambda l, j: (l, 0, j)),
        out_shape=jax.ShapeDtypeStruct((DEPTH, MOD_ROWS, n), F32),
        compiler_params=_params(2),
        name="modulation",
    )(cvec, w_mod, b_mod.reshape(DEPTH, 1, n))


class _Tokens:
    def __init__(self, rows, tm, latent):
        self.rows, self.tm, self.latent = rows, tm, latent

    def mod_row(self, i):
        return i // (SEQ // self.tm) if self.latent else CTX_MOD_ROW

    def pos_block(self, i):
        return i % (SEQ // self.tm)


def _mod_spec(tok, layer, chunk, tn):
    return pl.BlockSpec((None, None, 1, tn),
                        lambda i, j: (layer, tok.mod_row(i), 0, chunk * (D_MODEL // tn) + j))


def _normmod_kernel(x_ref, g_ref, sh_ref, sc_ref, o_ref):
    x = x_ref[...]
    ms = jnp.mean(x * x, axis=-1, keepdims=True)
    y = x * lax.rsqrt(ms + EPS) * g_ref[...]
    o_ref[...] = (y * (1.0 + sc_ref[...]) + sh_ref[...]).astype(o_ref.dtype)


def _normmod(x, g, mt, layer, chunk, tok):
    tm = 512
    t = _Tokens(tok.rows, tm, tok.latent)
    sh = _mod_spec(t, layer, chunk, D_MODEL)
    sc = _mod_spec(t, layer, chunk + 1, D_MODEL)
    return pl.pallas_call(
        _normmod_kernel,
        grid=(tok.rows // tm,),
        in_specs=[
            pl.BlockSpec((tm, D_MODEL), lambda i: (i, 0)),
            pl.BlockSpec((1, D_MODEL), lambda i: (0, 0)),
            pl.BlockSpec(sh.block_shape, lambda i: sh.index_map(i, 0)),
            pl.BlockSpec(sc.block_shape, lambda i: sc.index_map(i, 0)),
        ],
        out_specs=pl.BlockSpec((tm, D_MODEL), lambda i: (i, 0)),
        out_shape=jax.ShapeDtypeStruct((tok.rows, D_MODEL), BF),
        compiler_params=_params(1),
        name="normmod",
    )(x, g.reshape(1, D_MODEL), mt, mt)


def _mm_kernel(*refs, n_lhs, n_rhs, n_extra, n_out, groups, epilogue):
    lhs_refs = refs[:n_lhs]
    refs = refs[n_lhs:]
    rhs_refs = refs[:n_rhs]
    extra = refs[n_rhs:n_rhs + n_extra]
    out_refs = refs[n_rhs + n_extra:n_rhs + n_extra + n_out]
    acc_refs = refs[n_rhs + n_extra + n_out:]
    if n_lhs > 1:
        assert not acc_refs and n_rhs == 1
        for c0, c1 in groups:
            k0, acc = 0, None
            for l in lhs_refs:
                k1 = k0 + l.shape[1]
                d = jnp.dot(l[...], rhs_refs[0][k0:k1, c0:c1].astype(BF), preferred_element_type=F32)
                acc = d if acc is None else acc + d
                k0 = k1
            epilogue(c0, c1, [acc], extra, out_refs)
        return
    a = lhs_refs[0][...]
    if not acc_refs:
        for c0, c1 in groups:
            accs = [jnp.dot(a, r[:, c0:c1].astype(BF), preferred_element_type=F32) for r in rhs_refs]
            epilogue(c0, c1, accs, extra, out_refs)
        return

    @pl.when(pl.program_id(0) == 0)
    def _():
        for acc in acc_refs:
            acc[...] = jnp.zeros_like(acc)

    for c0, c1 in groups:
        epilogue(c0, c1, [acc[:, c0:c1] for acc in acc_refs], extra, out_refs)
    for c0, c1 in groups:
        for r, acc in zip(rhs_refs, acc_refs):
            acc[:, c0:c1] = jnp.dot(a, r[:, c0:c1].astype(BF), preferred_element_type=F32)


MXU_COLS = 256


def _matmul(name, lhs, rhs, extras, outs, epilogue, *, rows, tm, tn, n_cols, lhs_single=False, groups=None,
            lagged=False):
    if groups is None:
        groups = tuple((c, c + MXU_COLS) for c in range(0, tn, MXU_COLS))
    lhs = lhs if isinstance(lhs, (list, tuple)) else [lhs]
    kdim = sum(l.shape[1] for l in lhs)
    ni, nj = rows // tm, n_cols // tn
    if lagged:
        n_tiles = ni * nj
        grid = (n_tiles + 1,)
        cur = lambda f: (lambda t: f(*divmod(jnp.minimum(t, n_tiles - 1), nj)))
        prev = lambda f: (lambda t: f(*divmod(jnp.maximum(t - 1, 0), nj)))
        scratch = [pltpu.VMEM((tm, tn), F32) for _ in rhs]
    else:
        grid = (ni, nj)
        cur = prev = lambda f: f
        scratch = []
    lhs_mode = pl.Buffered(1) if lhs_single else None
    in_specs = [pl.BlockSpec((tm, l.shape[1]), cur(lambda i, j: (i, 0)), pipeline_mode=lhs_mode) for l in lhs]
    for _, prefix, off in rhs:
        in_specs.append(pl.BlockSpec((None,) * len(prefix) + (kdim, tn),
                                     cur(lambda i, j, prefix=prefix, off=off: prefix + (0, off + j))))
    in_specs += [pl.BlockSpec(spec.block_shape, prev(spec.index_map)) for _, spec in extras]
    kern = functools.partial(_mm_kernel, n_lhs=len(lhs), n_rhs=len(rhs), n_extra=len(extras),
                             n_out=len(outs), groups=groups, epilogue=epilogue)
    return pl.pallas_call(
        kern,
        grid=grid,
        in_specs=in_specs,
        out_specs=[pl.BlockSpec(spec.block_shape, prev(spec.index_map)) for _, _, spec in outs],
        out_shape=[jax.ShapeDtypeStruct(shape, dtype) for shape, dtype, _ in outs],
        scratch_shapes=scratch,
        compiler_params=_params(len(grid)),
        name=name,
    )(*lhs, *[r[0] for r in rhs], *[e[0] for e in extras])


def _epi_residual(coef):
    def epi(c0, c1, accs, extra, outs):
        res_ref, gate_ref = extra
        outs[0][:, c0:c1] = res_ref[:, c0:c1] + (coef * gate_ref[:, c0:c1]) * accs[0]
    return epi


def _rms_chunk(x, gain):
    ms = jnp.mean(x * x, axis=-1, keepdims=True)
    return x * lax.rsqrt(ms + EPS) * gain


def _rotate(x, tabs, shift):
    return x * tabs[0][...] + pltpu.roll(x, shift, 1) * tabs[1][...]


def _epi_heads(kinds, shift):
    def epi(c0, c1, accs, extra, outs):
        gain_ref, tabs = extra[0], extra[1:]
        for c in range(c0, c1, LANES):
            kind = kinds[c // LANES]
            x = accs[0][:, c - c0:c - c0 + LANES]
            if kind != "P":
                x = _rms_chunk(x, gain_ref[:, c:c + LANES])
            if kind == "NR":
                x = _rotate(x, tabs, shift)
            outs[0][:, c:c + LANES] = x.astype(outs[0].dtype)
    return epi


_MLA_DOWN_GROUPS = ((0, MLA_Q_RANK), (MLA_Q_RANK, MLA_Q_RANK + MLA_KV_RANK),
                    (MLA_Q_RANK + MLA_KV_RANK, MLA_DOWN_COLS))


def _epi_mla_down(rope):
    def epi(c0, c1, accs, extra, outs):
        which = [g[0] for g in _MLA_DOWN_GROUPS].index(c0)
        x = _rms_chunk(accs[0], extra[which][...])
        if rope and which == 2:
            x = _rotate(x, extra[3:], _ROPE64_SHIFT)
        outs[which][...] = x.astype(outs[which].dtype)
    return epi


def _axial_angles(rot_dim):
    t = jnp.arange(SEQ)
    row = (t // GRID_W).astype(F32)
    col = (t % GRID_W).astype(F32)
    axis_dim = rot_dim // 2
    inv_freq = ROPE_THETA ** (-jnp.arange(0, axis_dim, 2, dtype=F32) / axis_dim)
    ang = jnp.concatenate([row[:, None] * inv_freq, col[:, None] * inv_freq], axis=-1)
    return jnp.cos(ang), jnp.sin(ang)


_ROPE128_SHIFT = HEAD_DIM // 2
_ROPE64_SHIFT = LANES - MLA_ROPE // 2


def _rope128_tables():
    cos, sin = _axial_angles(HEAD_DIM)
    return [jnp.concatenate([cos, cos], -1), jnp.concatenate([-sin, sin], -1)]


def _rope64_tables():
    cos, sin = _axial_angles(MLA_ROPE)
    z64 = jnp.zeros((SEQ, LANES - MLA_ROPE), F32)
    return [jnp.concatenate([cos, cos, z64], -1), jnp.concatenate([-sin, sin, z64], -1)]


def _rope_extras(tabs, tok):
    return [(t, pl.BlockSpec((tok.tm, LANES), lambda i, j: (tok.pos_block(i), 0))) for t in tabs]


def _attn_kernel(*refs, n_seg, n_parts, group, dk, dv, tq, seg_lens, chunk):
    q_ref = refs[0]
    k_refs = refs[1:1 + n_seg * n_parts]
    v_refs = refs[1 + n_seg * n_parts:1 + n_seg * n_parts + n_seg]
    o_ref, k_scr, v_scr = refs[1 + n_seg * n_parts + n_seg:]

    @pl.when(pl.program_id(2) == 0)
    def _():
        off = 0
        for s in range(n_seg):
            length = seg_lens[s]
            col = 0
            for part in k_refs[s * n_parts:(s + 1) * n_parts]:
                width = part.shape[-1]
                k_scr[off:off + length, col:col + width] = part[...]
                col += width
            v_scr[off:off + length, :] = v_refs[s][...]
            off += length

    q = q_ref[...]
    if group > 1:
        q = jnp.concatenate([q[:, g * dk:(g + 1) * dk] for g in range(group)], axis=0)
    total = sum(seg_lens)
    m = l = acc = None
    for c0 in range(0, total, chunk):
        c1 = min(c0 + chunk, total)
        s = lax.dot_general(q, k_scr[c0:c1, :], (((1,), (1,)), ((), ())), preferred_element_type=F32)
        mc = jnp.max(s, axis=-1, keepdims=True)
        if m is None:
            m_new = mc
        else:
            m_new = jnp.maximum(m, mc)
            alpha = jnp.exp2(m - m_new)
        p = jnp.exp2(s - m_new)
        pv = jnp.dot(p.astype(BF), v_scr[c0:c1, :], preferred_element_type=F32)
        ps = jnp.sum(p, axis=-1, keepdims=True)
        if m is None:
            l, acc = ps, pv
        else:
            l, acc = alpha * l + ps, alpha * acc + pv
        m = m_new
    o = acc / l
    for g in range(group):
        o_ref[:, g * dv:(g + 1) * dv] = o[g * tq:(g + 1) * tq].astype(o_ref.dtype)


def _attention(name, q, segs, *, q_rows, n_kv, group, dk, dv, tq):
    nq = q_rows // tq
    n_parts = len(segs[0][1])
    in_specs = [pl.BlockSpec((tq, group * dk), lambda b, h, i: (b * nq + i, h))]
    operands = [q]
    for length, k_parts, _ in segs:
        for arr, col in k_parts:
            in_specs.append(pl.BlockSpec((length, LANES), lambda b, h, i, col=col: (b, col(h))))
            operands.append(arr)
    for length, _, (arr, col) in segs:
        in_specs.append(pl.BlockSpec((length, dv), lambda b, h, i, col=col: (b, col(h))))
        operands.append(arr)
    total = sum(s[0] for s in segs)
    kern = functools.partial(_attn_kernel, n_seg=len(segs), n_parts=n_parts, group=group, dk=dk, dv=dv,
                             tq=tq, seg_lens=tuple(s[0] for s in segs), chunk=ATTN_KEY_CHUNK)
    return pl.pallas_call(
        kern,
        grid=(BATCH, n_kv, nq),
        in_specs=in_specs,
        out_specs=pl.BlockSpec((tq, group * dv), lambda b, h, i: (b * nq + i, h)),
        out_shape=jax.ShapeDtypeStruct((BATCH * q_rows, n_kv * group * dv), BF),
        scratch_shapes=[pltpu.VMEM((total, dk), BF), pltpu.VMEM((total, dv), BF)],
        compiler_params=_params(3),
        name=name,
    )(*operands)


def _na_band_start(g):
    return min(max(NA_GROUP_ROWS * g - NA_WIN_H // 2, 0), GRID_H - NA_BAND_ROWS)


def _na_variant(g):
    return min(g, 1) + (g == NA_N_GROUPS - 1)


def _na_kernel(q_ref, k_ref, v_ref, kc_ref, vc_ref, bias_ref, o_ref):
    tq = NA_GROUP_ROWS * GRID_W
    band = NA_BAND_ROWS * GRID_W
    nt = (((1,), (1,)), ((), ()))
    kc = kc_ref[...]
    vc = vc_ref[...]
    for g in range(NA_N_GROUPS):
        start = _na_band_start(g) * GRID_W
        q = q_ref[g * tq:(g + 1) * tq, :]
        kb = k_ref[start:start + band, :]
        vb = v_ref[start:start + band, :]
        s_b = lax.dot_general(q, kb, nt, preferred_element_type=F32) + bias_ref[_na_variant(g)]
        s_c = lax.dot_general(q, kc, nt, preferred_element_type=F32)
        m = jnp.maximum(jnp.max(s_b, axis=-1, keepdims=True), jnp.max(s_c, axis=-1, keepdims=True))
        p_b = jnp.exp2(s_b - m)
        p_c = jnp.exp2(s_c - m)
        l = jnp.sum(p_b, axis=-1, keepdims=True) + jnp.sum(p_c, axis=-1, keepdims=True)
        o = (jnp.dot(p_b.astype(BF), vb, preferred_element_type=F32)
             + jnp.dot(p_c.astype(BF), vc, preferred_element_type=F32))
        o_ref[g * tq:(g + 1) * tq, :] = (o / l).astype(o_ref.dtype)


def _na_band_rows():
    ri = np.arange(NA_GROUP_ROWS)
    bi = np.arange(NA_BAND_ROWS)
    row_idx, row_ok = [], []
    for g in (0, 1, NA_N_GROUPS - 1):
        r = NA_GROUP_ROWS * g + ri
        r0 = np.clip(r - NA_WIN_H // 2, 0, GRID_H - NA_WIN_H)
        kr = _na_band_start(g) + bi
        row_ok.append((kr[None, :] >= r0[:, None]) & (kr[None, :] < r0[:, None] + NA_WIN_H))
        row_idx.append(np.clip(kr[None, :] - r[:, None] + NA_WIN_H - 1, 0, 2 * NA_WIN_H - 2))
    return np.stack(row_idx), np.stack(row_ok)


def _na_bias_kernel(f_ref, o_ref):
    _, row_ok = _na_band_rows()
    lane = lax.broadcasted_iota(jnp.int32, (GRID_W, LANES), 1)
    qc = lax.broadcasted_iota(jnp.int32, (GRID_W, LANES), 0)
    kc = lane % GRID_W
    left = lane < GRID_W
    c0 = jnp.clip(qc - NA_WIN_W // 2, 0, GRID_W - NA_WIN_W)
    col_ok = (kc >= c0) & (kc < c0 + NA_WIN_W)
    for v in range(3):
        for ri in range(NA_GROUP_ROWS):
            for bp in range(NA_BAND_ROWS // 2):
                halves = []
                for half in range(2):
                    row = (v * NA_GROUP_ROWS + ri) * NA_BAND_ROWS + 2 * bp + half
                    fb = jnp.broadcast_to(f_ref[row:row + 1, :], (GRID_W, LANES))
                    shift = (GRID_W * half - (GRID_W - 1)) % LANES
                    halves.append(pltpu.roll(fb, shift, 1, stride=1, stride_axis=0))
                ok_l, ok_r = bool(row_ok[v, ri, 2 * bp]), bool(row_ok[v, ri, 2 * bp + 1])
                ok = col_ok & ((left & ok_l) | (~left & ok_r))
                tile = jnp.where(left, halves[0], halves[1]) * LOG2E
                o_ref[v, ri * GRID_W:(ri + 1) * GRID_W, bp * LANES:(bp + 1) * LANES] = (
                    jnp.where(ok, tile, -jnp.inf))


def _na_bias_tables(rpb):
    row_idx, _ = _na_band_rows()
    n_rows = row_idx.size
    by_row = jnp.take(rpb, row_idx.reshape(-1), axis=1)
    pad = GRID_W - NA_WIN_W
    f = jnp.pad(by_row, ((0, 0), (0, 0), (pad, LANES - pad - by_row.shape[-1])))
    shape = (3, NA_GROUP_ROWS * GRID_W, NA_BAND_ROWS * GRID_W)
    return pl.pallas_call(
        _na_bias_kernel,
        grid=(NA_HEADS,),
        in_specs=[pl.BlockSpec((None, n_rows, LANES), lambda h: (h, 0, 0))],
        out_specs=pl.BlockSpec((None,) + shape, lambda h: (h, 0, 0, 0)),
        out_shape=jax.ShapeDtypeStruct((NA_HEADS,) + shape, F32),
        compiler_params=_params(1),
        name="na_bias",
    )(f)


def _na_attention(q, k, v, kc, vc, bias):
    head_block = lambda b, h: (b, h)
    return pl.pallas_call(
        _na_kernel,
        grid=(BATCH, NA_HEADS),
        in_specs=[
            pl.BlockSpec((SEQ, HEAD_DIM), head_block),
            pl.BlockSpec((SEQ, HEAD_DIM), head_block),
            pl.BlockSpec((SEQ, HEAD_DIM), head_block),
            pl.BlockSpec((CTX_LEN, HEAD_DIM), head_block),
            pl.BlockSpec((CTX_LEN, HEAD_DIM), head_block),
            pl.BlockSpec((None,) + bias.shape[1:], lambda b, h: (h, 0, 0, 0)),
        ],
        out_specs=pl.BlockSpec((SEQ, HEAD_DIM), head_block),
        out_shape=jax.ShapeDtypeStruct((BATCH * SEQ, NA_WIDTH), BF),
        compiler_params=_params(2),
        name="na_attention",
    )(q, k, v, kc, vc, bias)


def _head_gain(g, width):
    return jnp.tile(g, width // g.shape[0]).reshape(1, width)


def _ffn_up_kernel(*refs, cast_w2):
    if cast_w2:
        x_ref, wg_ref, wu_ref, w2_ref, o_ref, w2_bf_ref = refs
        w2_bf_ref[...] = w2_ref[...].astype(BF)
    else:
        x_ref, wg_ref, wu_ref, o_ref = refs
    x = x_ref[...]
    gate = jnp.dot(x, wg_ref[...].astype(BF), preferred_element_type=F32)
    up = jnp.dot(x, wu_ref[...].astype(BF), preferred_element_type=F32)
    o_ref[...] = (gate * jax.nn.sigmoid(gate) * up).astype(o_ref.dtype)


def _ffn_up(hn, w1, w2, layer, f, tok, cast_w2):
    tm = 2048 if tok.latent else tok.rows
    tn = 256
    ni, nj = tok.rows // tm, D_FF // tn
    in_specs = [
        pl.BlockSpec((tm, D_MODEL), lambda i, j: (i, 0), pipeline_mode=pl.Buffered(1)),
        pl.BlockSpec((None, None, D_MODEL, tn), lambda i, j: (layer, f, 0, j)),
        pl.BlockSpec((None, None, D_MODEL, tn), lambda i, j: (layer, f, 0, nj + j)),
    ]
    out_specs = [pl.BlockSpec((tm, tn), lambda i, j: (i, j))]
    out_shape = [jax.ShapeDtypeStruct((tok.rows, D_FF), BF)]
    operands = [hn, w1, w1]
    if cast_w2:
        assert D_FF % (ni * nj) == 0
        slab = D_FF // (ni * nj)
        in_specs.append(pl.BlockSpec((None, None, slab, D_MODEL), lambda i, j: (layer, f, i * nj + j, 0)))
        out_specs.append(pl.BlockSpec((slab, D_MODEL), lambda i, j: (i * nj + j, 0)))
        out_shape.append(jax.ShapeDtypeStruct((D_FF, D_MODEL), BF))
        operands.append(w2)
    return pl.pallas_call(
        functools.partial(_ffn_up_kernel, cast_w2=cast_w2),
        grid=(ni, nj),
        in_specs=in_specs,
        out_specs=out_specs,
        out_shape=out_shape,
        compiler_params=_params(2),
        name="ffn_up",
    )(*operands)


def _ffn(h, g, mt, layer, f, w1, w2, tok, w2_bf=None):
    chunk = 6 * f
    hn = _normmod(h, g, mt, layer, chunk, tok)
    if w2_bf is None:
        u, w2_bf = _ffn_up(hn, w1, w2, layer, f, tok, True)
    else:
        u, = _ffn_up(hn, w1, w2, layer, f, tok, False)
    tn = 512
    tile = pl.BlockSpec((tok.tm, tn), lambda i, j: (i, j))
    out, = _matmul(
        "ffn_down", u, [(w2_bf, (), 0)], [(h, tile), (mt, _mod_spec(tok, layer, chunk + 2, tn))],
        [((tok.rows, D_MODEL), F32, tile)], _epi_residual(0.5),
        rows=tok.rows, tm=tok.tm, tn=tn, n_cols=D_MODEL, lhs_single=True)
    return out, w2_bf


def _out_proj(name, o, w, prefix, h, mt, layer, tok):
    tn = 512
    tile = pl.BlockSpec((tok.tm, tn), lambda i, j: (i, j))
    out, = _matmul(
        name, o, [(w, prefix, 0)], [(h, tile), (mt, _mod_spec(tok, layer, 5, tn))],
        [((tok.rows, D_MODEL), F32, tile)], _epi_residual(1.0),
        rows=tok.rows, tm=tok.tm, tn=tn, n_cols=D_MODEL)
    return out


def _head_proj(name, u, w, prefix, col_off, width, kinds, gain, tabs, shift, tok, tn=512):
    out, = _matmul(
        name, u, [(w, prefix, col_off // tn)],
        [(gain, pl.BlockSpec((1, tn), lambda i, j: (0, j)))] + _rope_extras(tabs, tok),
        [((tok.rows, width), BF, pl.BlockSpec((tok.tm, tn), lambda i, j: (i, j)))],
        _epi_heads(kinds * (tn // (LANES * len(kinds))), shift),
        rows=tok.rows, tm=tok.tm, tn=tn, n_cols=width, lagged="NR" in kinds)
    return out


def _even_projections(u, w_in, j, gains, tok, want_q):
    na_q_g, na_k_g, gq_q_g, gq_k_g = gains
    scale = HEAD_DIM ** -0.5 * LOG2E
    rot = "NR" if tok.latent else "N"
    tabs = _rope128_tables() if tok.latent else []
    ones = jnp.ones((HEAD_DIM,), F32)
    proj = functools.partial(_head_proj, u=u, w=w_in, prefix=(j,), shift=_ROPE128_SHIFT, tok=tok)
    out = {}
    if want_q:
        out["na_q"] = proj("na_q", col_off=0, width=NA_WIDTH, kinds=("N",),
                           gain=_head_gain(na_q_g * scale, NA_WIDTH), tabs=[])
        out["gq_q"] = proj("gq_q", col_off=NA_WIDTH, width=GQA_Q_WIDTH, kinds=(rot,),
                           gain=_head_gain(gq_q_g * scale, GQA_Q_WIDTH), tabs=tabs)
    out["na_k"] = proj("na_k", col_off=EV_Q_COLS, width=NA_WIDTH, kinds=("N",),
                       gain=_head_gain(na_k_g, NA_WIDTH), tabs=[])
    out["na_v"] = proj("na_v", col_off=EV_Q_COLS + NA_WIDTH, width=NA_WIDTH, kinds=("P",),
                       gain=_head_gain(ones, NA_WIDTH), tabs=[])
    out["gq_k"] = proj("gq_k", col_off=EV_Q_COLS + 2 * NA_WIDTH, width=GQA_KV_WIDTH, kinds=(rot,),
                       gain=_head_gain(gq_k_g, GQA_KV_WIDTH), tabs=tabs)
    out["gq_v"] = proj("gq_v", col_off=EV_Q_COLS + 2 * NA_WIDTH + GQA_KV_WIDTH, width=GQA_KV_WIDTH,
                       kinds=("P",), gain=_head_gain(ones, GQA_KV_WIDTH), tabs=[])
    return out


def _even_mixer(ul, uc, hl, hc, mt, layer, w_in, w_out, j, gains, rpb, lat, ctx, ctx_out):
    pl_ = _even_projections(ul, w_in, j, gains, lat, True)
    pc = _even_projections(uc, w_in, j, gains, ctx, ctx_out)
    head = lambda h: h
    a = _na_attention(pl_["na_q"], pl_["na_k"], pl_["na_v"], pc["na_k"], pc["na_v"], _na_bias_tables(rpb))
    gq_segs = [(CTX_LEN, [(pc["gq_k"], head)], (pc["gq_v"], head)),
               (SEQ, [(pl_["gq_k"], head)], (pl_["gq_v"], head))]
    b = _attention("gqa", pl_["gq_q"], gq_segs, q_rows=SEQ, n_kv=GQA_KV_HEADS, group=GQA_GROUP,
                   dk=HEAD_DIM, dv=HEAD_DIM, tq=1024)
    hl = _out_proj("ev_out", [a, b], w_out, (j,), hl, mt, layer, lat)
    if not ctx_out:
        return hl, hc
    ac = _attention("na_ctx", pc["na_q"], [(CTX_LEN, [(pc["na_k"], head)], (pc["na_v"], head))],
                    q_rows=CTX_LEN, n_kv=NA_HEADS, group=1, dk=HEAD_DIM, dv=HEAD_DIM, tq=CTX_LEN)
    bc = _attention("gqa_ctx", pc["gq_q"], gq_segs[:1], q_rows=CTX_LEN, n_kv=GQA_KV_HEADS, group=GQA_GROUP,
                    dk=HEAD_DIM, dv=HEAD_DIM, tq=128)
    hc = _out_proj("ev_out_ctx", [ac, bc], w_out, (j,), hc, mt, layer, ctx)
    return hl, hc


def _mla_down(u, w_down_p, gains, tok):
    q_a_g, kv_a_g, kr_g = gains
    tabs = _rope64_tables() if tok.latent else []
    tm = 512
    t = _Tokens(tok.rows, tm, tok.latent)
    full = lambda n: pl.BlockSpec((1, n), lambda i, j: (0, 0))
    rows = lambda n: pl.BlockSpec((tm, n), lambda i, j: (i, 0))
    return _matmul(
        "mla_down", u, [(w_down_p, (), 0)],
        [(q_a_g.reshape(1, -1), full(MLA_Q_RANK)), (kv_a_g.reshape(1, -1), full(MLA_KV_RANK)),
         (kr_g, full(LANES))] + _rope_extras(tabs, t),
        [((tok.rows, MLA_Q_RANK), BF, rows(MLA_Q_RANK)), ((tok.rows, MLA_KV_RANK), BF, rows(MLA_KV_RANK)),
         ((tok.rows, LANES), BF, rows(LANES))],
        _epi_mla_down(tok.latent), rows=tok.rows, tm=tm, tn=MLA_DOWN_COLS, n_cols=MLA_DOWN_COLS,
        groups=_MLA_DOWN_GROUPS)


def _rope_slot_gain(g, rotated):
    return jnp.concatenate([g, g if rotated else jnp.zeros_like(g)])


def _mla_mixer(ul, uc, hl, hc, mt, layer, w, g, lat, ctx, ctx_out):
    scale = MLA_QK ** -0.5 * LOG2E
    kv_gain = _head_gain(jnp.concatenate([g["kn"], jnp.ones((MLA_V,), F32)]), MLA_HEADS * (MLA_NOPE + MLA_V))
    tabs = _rope64_tables()

    def down(u, tok):
        kr_gain = _rope_slot_gain(g["kr"], tok.latent).reshape(1, LANES)
        return _mla_down(u, w["down"], (g["q_a"], g["kv_a"], kr_gain), tok)

    def keys_values(ckv, tok):
        return _head_proj("mla_kv", ckv, w["ukv"], (), 0, MLA_HEADS * (MLA_NOPE + MLA_V), ("N", "P"),
                          kv_gain, [], 0, tok, tn=1024)

    def queries(name, cq, tok):
        gain = jnp.concatenate([g["qn"], _rope_slot_gain(g["qr"], tok.latent)]) * scale
        return _head_proj(name, cq, w["uq"], (), 0, MLA_HEADS * MLA_Q_SLOT,
                          ("N", "NR" if tok.latent else "N"), _head_gain(gain, MLA_HEADS * MLA_Q_SLOT),
                          tabs if tok.latent else [], _ROPE64_SHIFT, tok, tn=1024)

    cq_l, ckv_l, kr_l = down(ul, lat)
    cq_c, ckv_c, kr_c = down(uc, ctx)
    kv_l = keys_values(ckv_l, lat)
    kv_c = keys_values(ckv_c, ctx)
    q_l = queries("mla_q", cq_l, lat)
    nope = lambda h: 2 * h
    val = lambda h: 2 * h + 1
    shared = lambda h: 0
    segs = [(CTX_LEN, [(kv_c, nope), (kr_c, shared)], (kv_c, val)),
            (SEQ, [(kv_l, nope), (kr_l, shared)], (kv_l, val))]
    o = _attention("mla", q_l, segs, q_rows=SEQ, n_kv=MLA_HEADS, group=1, dk=MLA_Q_SLOT, dv=MLA_V, tq=4096)
    hl = _out_proj("mla_out", o, w["o"], (), hl, mt, layer, lat)
    if not ctx_out:
        return hl, hc
    q_c = queries("mla_q_ctx", cq_c, ctx)
    oc = _attention("mla_ctx", q_c, segs[:1], q_rows=CTX_LEN, n_kv=MLA_HEADS, group=1, dk=MLA_Q_SLOT,
                    dv=MLA_V, tq=CTX_LEN)
    hc = _out_proj("mla_out_ctx", oc, w["o"], (), hc, mt, layer, ctx)
    return hl, hc


def _mla_weights(w_down, w_uq, w_ukv, w_o, j):
    down = jnp.concatenate([w_down[j], w_down[j][:, -MLA_ROPE:]], axis=1).astype(BF)
    uq = w_uq[j].reshape(MLA_Q_RANK, MLA_HEADS, MLA_QK)
    uq = jnp.concatenate([uq, uq[:, :, -MLA_ROPE:]], axis=2).astype(BF)
    return {"down": down, "uq": uq.reshape(MLA_Q_RANK, MLA_HEADS * MLA_Q_SLOT), "ukv": w_ukv[j], "o": w_o[j]}


def kernel(x, c, ctx, c_ctx, norm_g, w_mod, b_mod, ffn_w1, ffn_w2, ev_w_in, ev_w_out, na_q_g, na_k_g, na_rpb, gq_q_g, gq_k_g, mla_w_down, mla_q_a_g, mla_kv_a_g, mla_w_uq, mla_w_ukv, mla_qn_g, mla_qr_g, mla_kn_g, mla_kr_g, mla_w_o):
    lat = _Tokens(BATCH * SEQ, 1024, True)
    ctk = _Tokens(BATCH * CTX_LEN, 512, False)
    cvec = jnp.zeros((MOD_ROWS, D_MODEL), F32).at[:BATCH].set(c).at[CTX_MOD_ROW].set(c_ctx)
    mt = _modulation(cvec, w_mod, b_mod).reshape(DEPTH, MOD_ROWS, 1, N_MOD * D_MODEL)
    hl = x.reshape(BATCH * SEQ, D_MODEL)
    hc = ctx.reshape(BATCH * CTX_LEN, D_MODEL)
    for i in range(DEPTH):
        last = i == DEPTH - 1
        g = norm_g[i]
        hl, w2_bf = _ffn(hl, g[0], mt, i, 0, ffn_w1, ffn_w2, lat)
        hc, _ = _ffn(hc, g[0], mt, i, 0, ffn_w1, ffn_w2, ctk, w2_bf)
        ul = _normmod(hl, g[1], mt, i, 3, lat)
        uc = _normmod(hc, g[1], mt, i, 3, ctk)
        j = i // 2
        if i % 2 == 0:
            gains = (na_q_g[j], na_k_g[j], gq_q_g[j], gq_k_g[j])
            hl, hc = _even_mixer(ul, uc, hl, hc, mt, i, ev_w_in, ev_w_out, j, gains, na_rpb[j], lat, ctk,
                                 not last)
        else:
            w = _mla_weights(mla_w_down, mla_w_uq, mla_w_ukv, mla_w_o, j)
            gains = {"q_a": mla_q_a_g[j], "kv_a": mla_kv_a_g[j], "qn": mla_qn_g[j], "qr": mla_qr_g[j],
                     "kn": mla_kn_g[j], "kr": mla_kr_g[j]}
            hl, hc = _mla_mixer(ul, uc, hl, hc, mt, i, w, gains, lat, ctk, not last)
        hl, w2_bf = _ffn(hl, g[2], mt, i, 1, ffn_w1, ffn_w2, lat)
        if not last:
            hc, _ = _ffn(hc, g[2], mt, i, 1, ffn_w1, ffn_w2, ctk, w2_bf)
    return hl.reshape(BATCH, SEQ, D_MODEL)
```

```python
import functools

import numpy as np
import jax
import jax.numpy as jnp
from jax import lax
from jax.experimental import pallas as pl
from jax.experimental.pallas import tpu as pltpu

D_MODEL = 4096
BATCH = 2
SEQ = 4096
DEPTH = 2
CTX_LEN = 256
GRID_W = 64
GRID_H = SEQ // GRID_W
HEAD_DIM = 128
EPS = 1e-6
ROPE_THETA = 10000.0
N_MOD = 9
D_FF = 10240
NA_HEADS = 16
NA_WIN_H = 8
NA_WIN_W = 16
GQA_Q_HEADS = 16
GQA_KV_HEADS = 4
GQA_GROUP = GQA_Q_HEADS // GQA_KV_HEADS
NA_WIDTH = NA_HEADS * HEAD_DIM
GQA_Q_WIDTH = GQA_Q_HEADS * HEAD_DIM
GQA_KV_WIDTH = GQA_KV_HEADS * HEAD_DIM
EV_Q_COLS = NA_WIDTH + GQA_Q_WIDTH
MLA_HEADS = 32
MLA_Q_RANK = 1024
MLA_KV_RANK = 512
MLA_NOPE = 128
MLA_ROPE = 64
MLA_V = 128
MLA_QK = MLA_NOPE + MLA_ROPE

LANES = 128
MLA_Q_SLOT = 2 * LANES
MLA_DOWN_COLS = MLA_Q_RANK + MLA_KV_RANK + LANES
MOD_ROWS = 8
CTX_MOD_ROW = BATCH

NA_GROUP_ROWS = 4
NA_BAND_ROWS = 12
NA_N_GROUPS = GRID_H // NA_GROUP_ROWS

ATTN_KEY_CHUNK = 256
LOG2E = 1.4426950408889634

VMEM_LIMIT = 56 * 1024 * 1024

BF = jnp.bfloat16
F32 = jnp.float32


def _params(n_axes):
    return pltpu.CompilerParams(dimension_semantics=("arbitrary",) * n_axes, vmem_limit_bytes=VMEM_LIMIT)


def _mod_kernel(c_ref, w_ref, b_ref, o_ref):
    c = c_ref[...]
    a = (c * jax.nn.sigmoid(c)).astype(BF)
    o_ref[...] = jnp.dot(a, w_ref[...].astype(BF), preferred_element_type=F32) + b_ref[...]


def _modulation(cvec, w_mod, b_mod):
    n = N_MOD * D_MODEL
    tn = 512
    return pl.pallas_call(
        _mod_kernel,
        grid=(DEPTH, n // tn),
        in_specs=[
            pl.BlockSpec((MOD_ROWS, D_MODEL), lambda l, j: (0, 0)),
            pl.BlockSpec((None, D_MODEL, tn), lambda l, j: (l, 0, j)),
            pl.BlockSpec((None, 1, tn), lambda l, j: (l, 0, j)),
        ],
        out_specs=pl.BlockSpec((None, MOD_ROWS, tn), lambda l, j: (l, 0, j)),
        out_shape=jax.ShapeDtypeStruct((DEPTH, MOD_ROWS, n), F32),
        compiler_params=_params(2),
        name="modulation",
    )(cvec, w_mod, b_mod.reshape(DEPTH, 1, n))


class _Tokens:
    def __init__(self, rows, tm, latent):
        self.rows, self.tm, self.latent = rows, tm, latent

    def mod_row(self, i):
        return i // (SEQ // self.tm) if self.latent else CTX_MOD_ROW

    def pos_block(self, i):
        return i % (SEQ // self.tm)


def _mod_spec(tok, layer, chunk, tn):
    return pl.BlockSpec((None, None, 1, tn),
                        lambda i, j: (layer, tok.mod_row(i), 0, chunk * (D_MODEL // tn) + j))


def _normmod_kernel(x_ref, g_ref, sh_ref, sc_ref, o_ref):
    x = x_ref[...]
    ms = jnp.mean(x * x, axis=-1, keepdims=True)
    y = x * lax.rsqrt(ms + EPS) * g_ref[...]
    o_ref[...] = (y * (1.0 + sc_ref[...]) + sh_ref[...]).astype(o_ref.dtype)


def _normmod(x, g, mt, layer, chunk, tok):
    tm = 512
    t = _Tokens(tok.rows, tm, tok.latent)
    sh = _mod_spec(t, layer, chunk, D_MODEL)
    sc = _mod_spec(t, layer, chunk + 1, D_MODEL)
    return pl.pallas_call(
        _normmod_kernel,
        grid=(tok.rows // tm,),
        in_specs=[
            pl.BlockSpec((tm, D_MODEL), lambda i: (i, 0)),
            pl.BlockSpec((1, D_MODEL), lambda i: (0, 0)),
            pl.BlockSpec(sh.block_shape, lambda i: sh.index_map(i, 0)),
            pl.BlockSpec(sc.block_shape, lambda i: sc.index_map(i, 0)),
        ],
        out_specs=pl.BlockSpec((tm, D_MODEL), lambda i: (i, 0)),
        out_shape=jax.ShapeDtypeStruct((tok.rows, D_MODEL), BF),
        compiler_params=_params(1),
        name="normmod",
    )(x, g.reshape(1, D_MODEL), mt, mt)


def _mm_kernel(*refs, n_lhs, n_rhs, n_extra, n_out, groups, epilogue):
    lhs_refs = refs[:n_lhs]
    refs = refs[n_lhs:]
    rhs_refs = refs[:n_rhs]
    extra = refs[n_rhs:n_rhs + n_extra]
    out_refs = refs[n_rhs + n_extra:n_rhs + n_extra + n_out]
    acc_refs = refs[n_rhs + n_extra + n_out:]
    if n_lhs > 1:
        assert not acc_refs and n_rhs == 1
        for c0, c1 in groups:
            k0, acc = 0, None
            for l in lhs_refs:
                k1 = k0 + l.shape[1]
                d = jnp.dot(l[...], rhs_refs[0][k0:k1, c0:c1].astype(BF), preferred_element_type=F32)
                acc = d if acc is None else acc + d
                k0 = k1
            epilogue(c0, c1, [acc], extra, out_refs)
        return
    a = lhs_refs[0][...]
    if not acc_refs:
        for c0, c1 in groups:
            accs = [jnp.dot(a, r[:, c0:c1].astype(BF), preferred_element_type=F32) for r in rhs_refs]
            epilogue(c0, c1, accs, extra, out_refs)
        return

    @pl.when(pl.program_id(0) == 0)
    def _():
        for acc in acc_refs:
            acc[...] = jnp.zeros_like(acc)

    for c0, c1 in groups:
        epilogue(c0, c1, [acc[:, c0:c1] for acc in acc_refs], extra, out_refs)
    for c0, c1 in groups:
        for r, acc in zip(rhs_refs, acc_refs):
            acc[:, c0:c1] = jnp.dot(a, r[:, c0:c1].astype(BF), preferred_element_type=F32)


MXU_COLS = 256


def _matmul(name, lhs, rhs, extras, outs, epilogue, *, rows, tm, tn, n_cols, lhs_single=False, groups=None,
            lagged=False):
    if groups is None:
        groups = tuple((c, c + MXU_COLS) for c in range(0, tn, MXU_COLS))
    lhs = lhs if isinstance(lhs, (list, tuple)) else [lhs]
    kdim = sum(l.shape[1] for l in lhs)
    ni, nj = rows // tm, n_cols // tn
    if lagged:
        n_tiles = ni * nj
        grid = (n_tiles + 1,)
        cur = lambda f: (lambda t: f(*divmod(jnp.minimum(t, n_tiles - 1), nj)))
        prev = lambda f: (lambda t: f(*divmod(jnp.maximum(t - 1, 0), nj)))
        scratch = [pltpu.VMEM((tm, tn), F32) for _ in rhs]
    else:
        grid = (ni, nj)
        cur = prev = lambda f: f
        scratch = []
    lhs_mode = pl.Buffered(1) if lhs_single else None
    in_specs = [pl.BlockSpec((tm, l.shape[1]), cur(lambda i, j: (i, 0)), pipeline_mode=lhs_mode) for l in lhs]
    for _, prefix, off in rhs:
        in_specs.append(pl.BlockSpec((None,) * len(prefix) + (kdim, tn),
                                     cur(lambda i, j, prefix=prefix, off=off: prefix + (0, off + j))))
    in_specs += [pl.BlockSpec(spec.block_shape, prev(spec.index_map)) for _, spec in extras]
    kern = functools.partial(_mm_kernel, n_lhs=len(lhs), n_rhs=len(rhs), n_extra=len(extras),
                             n_out=len(outs), groups=groups, epilogue=epilogue)
    return pl.pallas_call(
        kern,
        grid=grid,
        in_specs=in_specs,
        out_specs=[pl.BlockSpec(spec.block_shape, prev(spec.index_map)) for _, _, spec in outs],
        out_shape=[jax.ShapeDtypeStruct(shape, dtype) for shape, dtype, _ in outs],
        scratch_shapes=scratch,
        compiler_params=_params(len(grid)),
        name=name,
    )(*lhs, *[r[0] for r in rhs], *[e[0] for e in extras])


def _epi_residual(coef):
    def epi(c0, c1, accs, extra, outs):
        res_ref, gate_ref = extra
        outs[0][:, c0:c1] = res_ref[:, c0:c1] + (coef * gate_ref[:, c0:c1]) * accs[0]
    return epi


def _rms_chunk(x, gain):
    ms = jnp.mean(x * x, axis=-1, keepdims=True)
    return x * lax.rsqrt(ms + EPS) * gain


def _rotate(x, tabs, shift):
    return x * tabs[0][...] + pltpu.roll(x, shift, 1) * tabs[1][...]


def _epi_heads(kinds, shift):
    def epi(c0, c1, accs, extra, outs):
        gain_ref, tabs = extra[0], extra[1:]
        for c in range(c0, c1, LANES):
            kind = kinds[c // LANES]
            x = accs[0][:, c - c0:c - c0 + LANES]
            if kind != "P":
                x = _rms_chunk(x, gain_ref[:, c:c + LANES])
            if kind == "NR":
                x = _rotate(x, tabs, shift)
            outs[0][:, c:c + LANES] = x.astype(outs[0].dtype)
    return epi


_MLA_DOWN_GROUPS = ((0, MLA_Q_RANK), (MLA_Q_RANK, MLA_Q_RANK + MLA_KV_RANK),
                    (MLA_Q_RANK + MLA_KV_RANK, MLA_DOWN_COLS))


def _epi_mla_down(rope):
    def epi(c0, c1, accs, extra, outs):
        which = [g[0] for g in _MLA_DOWN_GROUPS].index(c0)
        x = _rms_chunk(accs[0], extra[which][...])
        if rope and which == 2:
            x = _rotate(x, extra[3:], _ROPE64_SHIFT)
        outs[which][...] = x.astype(outs[which].dtype)
    return epi


def _axial_angles(rot_dim):
    t = jnp.arange(SEQ)
    row = (t // GRID_W).astype(F32)
    col = (t % GRID_W).astype(F32)
    axis_dim = rot_dim // 2
    inv_freq = ROPE_THETA ** (-jnp.arange(0, axis_dim, 2, dtype=F32) / axis_dim)
    ang = jnp.concatenate([row[:, None] * inv_freq, col[:, None] * inv_freq], axis=-1)
    return jnp.cos(ang), jnp.sin(ang)


_ROPE128_SHIFT = HEAD_DIM // 2
_ROPE64_SHIFT = LANES - MLA_ROPE // 2


def _rope128_tables():
    cos, sin = _axial_angles(HEAD_DIM)
    return [jnp.concatenate([cos, cos], -1), jnp.concatenate([-sin, sin], -1)]


def _rope64_tables():
    cos, sin = _axial_angles(MLA_ROPE)
    z64 = jnp.zeros((SEQ, LANES - MLA_ROPE), F32)
    return [jnp.concatenate([cos, cos, z64], -1), jnp.concatenate([-sin, sin, z64], -1)]


def _rope_extras(tabs, tok):
    return [(t, pl.BlockSpec((tok.tm, LANES), lambda i, j: (tok.pos_block(i), 0))) for t in tabs]


def _attn_kernel(*refs, n_seg, n_parts, group, dk, dv, tq, seg_lens, chunk):
    q_ref = refs[0]
    k_refs = refs[1:1 + n_seg * n_parts]
    v_refs = refs[1 + n_seg * n_parts:1 + n_seg * n_parts + n_seg]
    o_ref, k_scr, v_scr = refs[1 + n_seg * n_parts + n_seg:]

    @pl.when(pl.program_id(2) == 0)
    def _():
        off = 0
        for s in range(n_seg):
            length = seg_lens[s]
            col = 0
            for part in k_refs[s * n_parts:(s + 1) * n_parts]:
                width = part.shape[-1]
                k_scr[off:off + length, col:col + width] = part[...]
                col += width
            v_scr[off:off + length, :] = v_refs[s][...]
            off += length

    q = q_ref[...]
    if group > 1:
        q = jnp.concatenate([q[:, g * dk:(g + 1) * dk] for g in range(group)], axis=0)
    total = sum(seg_lens)
    m = l = acc = None
    for c0 in range(0, total, chunk):
        c1 = min(c0 + chunk, total)
        s = lax.dot_general(q, k_scr[c0:c1, :], (((1,), (1,)), ((), ())), preferred_element_type=F32)
        mc = jnp.max(s, axis=-1, keepdims=True)
        if m is None:
            m_new = mc
        else:
            m_new = jnp.maximum(m, mc)
            alpha = jnp.exp2(m - m_new)
        p = jnp.exp2(s - m_new)
        pv = jnp.dot(p.astype(BF), v_scr[c0:c1, :], preferred_element_type=F32)
        ps = jnp.sum(p, axis=-1, keepdims=True)
        if m is None:
            l, acc = ps, pv
        else:
            l, acc = alpha * l + ps, alpha * acc + pv
        m = m_new
    o = acc / l
    for g in range(group):
        o_ref[:, g * dv:(g + 1) * dv] = o[g * tq:(g + 1) * tq].astype(o_ref.dtype)


def _attention(name, q, segs, *, q_rows, n_kv, group, dk, dv, tq):
    nq = q_rows // tq
    n_parts = len(segs[0][1])
    in_specs = [pl.BlockSpec((tq, group * dk), lambda b, h, i: (b * nq + i, h))]
    operands = [q]
    for length, k_parts, _ in segs:
        for arr, col in k_parts:
            in_specs.append(pl.BlockSpec((length, LANES), lambda b, h, i, col=col: (b, col(h))))
            operands.append(arr)
    for length, _, (arr, col) in segs:
        in_specs.append(pl.BlockSpec((length, dv), lambda b, h, i, col=col: (b, col(h))))
        operands.append(arr)
    total = sum(s[0] for s in segs)
    kern = functools.partial(_attn_kernel, n_seg=len(segs), n_parts=n_parts, group=group, dk=dk, dv=dv,
                             tq=tq, seg_lens=tuple(s[0] for s in segs), chunk=ATTN_KEY_CHUNK)
    return pl.pallas_call(
        kern,
        grid=(BATCH, n_kv, nq),
        in_specs=in_specs,
        out_specs=pl.BlockSpec((tq, group * dv), lambda b, h, i: (b * nq + i, h)),
        out_shape=jax.ShapeDtypeStruct((BATCH * q_rows, n_kv * group * dv), BF),
        scratch_shapes=[pltpu.VMEM((total, dk), BF), pltpu.VMEM((total, dv), BF)],
        compiler_params=_params(3),
        name=name,
    )(*operands)


def _na_band_start(g):
    return min(max(NA_GROUP_ROWS * g - NA_WIN_H // 2, 0), GRID_H - NA_BAND_ROWS)


def _na_variant(g):
    return min(g, 1) + (g == NA_N_GROUPS - 1)


def _na_kernel(q_ref, k_ref, v_ref, kc_ref, vc_ref, bias_ref, o_ref):
    tq = NA_GROUP_ROWS * GRID_W
    band = NA_BAND_ROWS * GRID_W
    nt = (((1,), (1,)), ((), ()))
    vc = vc_ref[...]
    s_ctx = lax.dot_general(q_ref[...], kc_ref[...], nt, preferred_element_type=F32)
    for g in range(NA_N_GROUPS):
        start = _na_band_start(g) * GRID_W
        q = q_ref[g * tq:(g + 1) * tq, :]
        kb = k_ref[start:start + band, :]
        vb = v_ref[start:start + band, :]
        s_b = lax.dot_general(q, kb, nt, preferred_element_type=F32) + bias_ref[_na_variant(g)]
        s_c = s_ctx[g * tq:(g + 1) * tq]
        m = jnp.maximum(jnp.max(s_b, axis=-1, keepdims=True), jnp.max(s_c, axis=-1, keepdims=True))
        p_b = jnp.exp2(s_b - m)
        p_c = jnp.exp2(s_c - m)
        l = jnp.sum(p_b, axis=-1, keepdims=True) + jnp.sum(p_c, axis=-1, keepdims=True)
        o = (jnp.dot(p_b.astype(BF), vb, preferred_element_type=F32)
             + jnp.dot(p_c.astype(BF), vc, preferred_element_type=F32))
        o_ref[g * tq:(g + 1) * tq, :] = (o / l).astype(o_ref.dtype)


def _na_band_rows():
    ri = np.arange(NA_GROUP_ROWS)
    bi = np.arange(NA_BAND_ROWS)
    row_idx, row_ok = [], []
    for g in (0, 1, NA_N_GROUPS - 1):
        r = NA_GROUP_ROWS * g + ri
        r0 = np.clip(r - NA_WIN_H // 2, 0, GRID_H - NA_WIN_H)
        kr = _na_band_start(g) + bi
        row_ok.append((kr[None, :] >= r0[:, None]) & (kr[None, :] < r0[:, None] + NA_WIN_H))
        row_idx.append(np.clip(kr[None, :] - r[:, None] + NA_WIN_H - 1, 0, 2 * NA_WIN_H - 2))
    return np.stack(row_idx), np.stack(row_ok)


def _na_bias_kernel(f_ref, o_ref):
    _, row_ok = _na_band_rows()
    lane = lax.broadcasted_iota(jnp.int32, (GRID_W, LANES), 1)
    qc = lax.broadcasted_iota(jnp.int32, (GRID_W, LANES), 0)
    kc = lane % GRID_W
    left = lane < GRID_W
    c0 = jnp.clip(qc - NA_WIN_W // 2, 0, GRID_W - NA_WIN_W)
    col_ok = (kc >= c0) & (kc < c0 + NA_WIN_W)
    for v in range(3):
        for ri in range(NA_GROUP_ROWS):
            for bp in range(NA_BAND_ROWS // 2):
                halves = []
                for half in range(2):
                    row = (v * NA_GROUP_ROWS + ri) * NA_BAND_ROWS + 2 * bp + half
                    fb = jnp.broadcast_to(f_ref[row:row + 1, :], (GRID_W, LANES))
                    shift = (GRID_W * half - (GRID_W - 1)) % LANES
                    halves.append(pltpu.roll(fb, shift, 1, stride=1, stride_axis=0))
                ok_l, ok_r = bool(row_ok[v, ri, 2 * bp]), bool(row_ok[v, ri, 2 * bp + 1])
                ok = col_ok & ((left & ok_l) | (~left & ok_r))
                tile = jnp.where(left, halves[0], halves[1]) * LOG2E
                o_ref[v, ri * GRID_W:(ri + 1) * GRID_W, bp * LANES:(bp + 1) * LANES] = (
                    jnp.where(ok, tile, -jnp.inf))


def _na_bias_tables(rpb):
    row_idx, _ = _na_band_rows()
    n_rows = row_idx.size
    by_row = jnp.take(rpb, row_idx.reshape(-1), axis=1)
    pad = GRID_W - NA_WIN_W
    f = jnp.pad(by_row, ((0, 0), (0, 0), (pad, LANES - pad - by_row.shape[-1])))
    shape = (3, NA_GROUP_ROWS * GRID_W, NA_BAND_ROWS * GRID_W)
    return pl.pallas_call(
        _na_bias_kernel,
        grid=(NA_HEADS,),
        in_specs=[pl.BlockSpec((None, n_rows, LANES), lambda h: (h, 0, 0))],
        out_specs=pl.BlockSpec((None,) + shape, lambda h: (h, 0, 0, 0)),
        out_shape=jax.ShapeDtypeStruct((NA_HEADS,) + shape, F32),
        compiler_params=_params(1),
        name="na_bias",
    )(f)


def _na_attention(q, k, v, kc, vc, bias):
    head_block = lambda b, h: (b, h)
    return pl.pallas_call(
        _na_kernel,
        grid=(BATCH, NA_HEADS),
        in_specs=[
            pl.BlockSpec((SEQ, HEAD_DIM), head_block),
            pl.BlockSpec((SEQ, HEAD_DIM), head_block),
            pl.BlockSpec((SEQ, HEAD_DIM), head_block),
            pl.BlockSpec((CTX_LEN, HEAD_DIM), head_block),
            pl.BlockSpec((CTX_LEN, HEAD_DIM), head_block),
            pl.BlockSpec((None,) + bias.shape[1:], lambda b, h: (h, 0, 0, 0)),
        ],
        out_specs=pl.BlockSpec((SEQ, HEAD_DIM), head_block),
        out_shape=jax.ShapeDtypeStruct((BATCH * SEQ, NA_WIDTH), BF),
        compiler_params=_params(2),
        name="na_attention",
    )(q, k, v, kc, vc, bias)


def _head_gain(g, width):
    return jnp.tile(g, width // g.shape[0]).reshape(1, width)


def _ffn_up_kernel(*refs, cast_w2):
    if cast_w2:
        x_ref, wg_ref, wu_ref, w2_ref, o_ref, w2_bf_ref = refs
        w2_bf_ref[...] = w2_ref[...].astype(BF)
    else:
        x_ref, wg_ref, wu_ref, o_ref = refs
    x = x_ref[...]
    gate = jnp.dot(x, wg_ref[...].astype(BF), preferred_element_type=F32)
    up = jnp.dot(x, wu_ref[...].astype(BF), preferred_element_type=F32)
    o_ref[...] = (gate * jax.nn.sigmoid(gate) * up).astype(o_ref.dtype)


def _ffn_up(hn, w1, w2, layer, f, tok, cast_w2):
    tm = 2048 if tok.latent else tok.rows
    tn = 256
    ni, nj = tok.rows // tm, D_FF // tn
    in_specs = [
        pl.BlockSpec((tm, D_MODEL), lambda i, j: (i, 0), pipeline_mode=pl.Buffered(1)),
        pl.BlockSpec((None, None, D_MODEL, tn), lambda i, j: (layer, f, 0, j)),
        pl.BlockSpec((None, None, D_MODEL, tn), lambda i, j: (layer, f, 0, nj + j)),
    ]
    out_specs = [pl.BlockSpec((tm, tn), lambda i, j: (i, j))]
    out_shape = [jax.ShapeDtypeStruct((tok.rows, D_FF), BF)]
    operands = [hn, w1, w1]
    if cast_w2:
        assert D_FF % (ni * nj) == 0
        slab = D_FF // (ni * nj)
        in_specs.append(pl.BlockSpec((None, None, slab, D_MODEL), lambda i, j: (layer, f, i * nj + j, 0)))
        out_specs.append(pl.BlockSpec((slab, D_MODEL), lambda i, j: (i * nj + j, 0)))
        out_shape.append(jax.ShapeDtypeStruct((D_FF, D_MODEL), BF))
        operands.append(w2)
    return pl.pallas_call(
        functools.partial(_ffn_up_kernel, cast_w2=cast_w2),
        grid=(ni, nj),
        in_specs=in_specs,
        out_specs=out_specs,
        out_shape=out_shape,
        compiler_params=_params(2),
        name="ffn_up",
    )(*operands)


def _ffn(h, g, mt, layer, f, w1, w2, tok, w2_bf=None):
    chunk = 6 * f
    hn = _normmod(h, g, mt, layer, chunk, tok)
    if w2_bf is None:
        u, w2_bf = _ffn_up(hn, w1, w2, layer, f, tok, True)
    else:
        u, = _ffn_up(hn, w1, w2, layer, f, tok, False)
    tn = 512
    tile = pl.BlockSpec((tok.tm, tn), lambda i, j: (i, j))
    out, = _matmul(
        "ffn_down", u, [(w2_bf, (), 0)], [(h, tile), (mt, _mod_spec(tok, layer, chunk + 2, tn))],
        [((tok.rows, D_MODEL), F32, tile)], _epi_residual(0.5),
        rows=tok.rows, tm=tok.tm, tn=tn, n_cols=D_MODEL, lhs_single=True)
    return out, w2_bf


def _out_proj(name, o, w, prefix, h, mt, layer, tok):
    tn = 512
    tile = pl.BlockSpec((tok.tm, tn), lambda i, j: (i, j))
    out, = _matmul(
        name, o, [(w, prefix, 0)], [(h, tile), (mt, _mod_spec(tok, layer, 5, tn))],
        [((tok.rows, D_MODEL), F32, tile)], _epi_residual(1.0),
        rows=tok.rows, tm=tok.tm, tn=tn, n_cols=D_MODEL)
    return out


def _head_proj(name, u, w, prefix, col_off, width, kinds, gain, tabs, shift, tok, tn=512):
    out, = _matmul(
        name, u, [(w, prefix, col_off // tn)],
        [(gain, pl.BlockSpec((1, tn), lambda i, j: (0, j)))] + _rope_extras(tabs, tok),
        [((tok.rows, width), BF, pl.BlockSpec((tok.tm, tn), lambda i, j: (i, j)))],
        _epi_heads(kinds * (tn // (LANES * len(kinds))), shift),
        rows=tok.rows, tm=tok.tm, tn=tn, n_cols=width, lagged="NR" in kinds)
    return out


def _even_projections(u, w_in, j, gains, tok, want_q):
    na_q_g, na_k_g, gq_q_g, gq_k_g = gains
    scale = HEAD_DIM ** -0.5 * LOG2E
    rot = "NR" if tok.latent else "N"
    tabs = _rope128_tables() if tok.latent else []
    ones = jnp.ones((HEAD_DIM,), F32)
    proj = functools.partial(_head_proj, u=u, w=w_in, prefix=(j,), shift=_ROPE128_SHIFT, tok=tok)
    out = {}
    if want_q:
        out["na_q"] = proj("na_q", col_off=0, width=NA_WIDTH, kinds=("N",),
                           gain=_head_gain(na_q_g * scale, NA_WIDTH), tabs=[])
        out["gq_q"] = proj("gq_q", col_off=NA_WIDTH, width=GQA_Q_WIDTH, kinds=(rot,),
                           gain=_head_gain(gq_q_g * scale, GQA_Q_WIDTH), tabs=tabs)
    out["na_k"] = proj("na_k", col_off=EV_Q_COLS, width=NA_WIDTH, kinds=("N",),
                       gain=_head_gain(na_k_g, NA_WIDTH), tabs=[])
    out["na_v"] = proj("na_v", col_off=EV_Q_COLS + NA_WIDTH, width=NA_WIDTH, kinds=("P",),
                       gain=_head_gain(ones, NA_WIDTH), tabs=[])
    out["gq_k"] = proj("gq_k", col_off=EV_Q_COLS + 2 * NA_WIDTH, width=GQA_KV_WIDTH, kinds=(rot,),
                       gain=_head_gain(gq_k_g, GQA_KV_WIDTH), tabs=tabs)
    out["gq_v"] = proj("gq_v", col_off=EV_Q_COLS + 2 * NA_WIDTH + GQA_KV_WIDTH, width=GQA_KV_WIDTH,
                       kinds=("P",), gain=_head_gain(ones, GQA_KV_WIDTH), tabs=[])
    return out


def _even_mixer(ul, uc, hl, hc, mt, layer, w_in, w_out, j, gains, rpb, lat, ctx, ctx_out):
    pl_ = _even_projections(ul, w_in, j, gains, lat, True)
    pc = _even_projections(uc, w_in, j, gains, ctx, ctx_out)
    head = lambda h: h
    a = _na_attention(pl_["na_q"], pl_["na_k"], pl_["na_v"], pc["na_k"], pc["na_v"], _na_bias_tables(rpb))
    gq_segs = [(CTX_LEN, [(pc["gq_k"], head)], (pc["gq_v"], head)),
               (SEQ, [(pl_["gq_k"], head)], (pl_["gq_v"], head))]
    b = _attention("gqa", pl_["gq_q"], gq_segs, q_rows=SEQ, n_kv=GQA_KV_HEADS, group=GQA_GROUP,
                   dk=HEAD_DIM, dv=HEAD_DIM, tq=1024)
    hl = _out_proj("ev_out", [a, b], w_out, (j,), hl, mt, layer, lat)
    if not ctx_out:
        return hl, hc
    ac = _attention("na_ctx", pc["na_q"], [(CTX_LEN, [(pc["na_k"], head)], (pc["na_v"], head))],
                    q_rows=CTX_LEN, n_kv=NA_HEADS, group=1, dk=HEAD_DIM, dv=HEAD_DIM, tq=CTX_LEN)
    bc = _attention("gqa_ctx", pc["gq_q"], gq_segs[:1], q_rows=CTX_LEN, n_kv=GQA_KV_HEADS, group=GQA_GROUP,
                    dk=HEAD_DIM, dv=HEAD_DIM, tq=128)
    hc = _out_proj("ev_out_ctx", [ac, bc], w_out, (j,), hc, mt, layer, ctx)
    return hl, hc


def _mla_down(u, w_down_p, gains, tok):
    q_a_g, kv_a_g, kr_g = gains
    tabs = _rope64_tables() if tok.latent else []
    tm = 512
    t = _Tokens(tok.rows, tm, tok.latent)
    full = lambda n: pl.BlockSpec((1, n), lambda i, j: (0, 0))
    rows = lambda n: pl.BlockSpec((tm, n), lambda i, j: (i, 0))
    return _matmul(
        "mla_down", u, [(w_down_p, (), 0)],
        [(q_a_g.reshape(1, -1), full(MLA_Q_RANK)), (kv_a_g.reshape(1, -1), full(MLA_KV_RANK)),
         (kr_g, full(LANES))] + _rope_extras(tabs, t),
        [((tok.rows, MLA_Q_RANK), BF, rows(MLA_Q_RANK)), ((tok.rows, MLA_KV_RANK), BF, rows(MLA_KV_RANK)),
         ((tok.rows, LANES), BF, rows(LANES))],
        _epi_mla_down(tok.latent), rows=tok.rows, tm=tm, tn=MLA_DOWN_COLS, n_cols=MLA_DOWN_COLS,
        groups=_MLA_DOWN_GROUPS)


def _rope_slot_gain(g, rotated):
    return jnp.concatenate([g, g if rotated else jnp.zeros_like(g)])


def _mla_mixer(ul, uc, hl, hc, mt, layer, w, g, lat, ctx, ctx_out):
    scale = MLA_QK ** -0.5 * LOG2E
    kv_gain = _head_gain(jnp.concatenate([g["kn"], jnp.ones((MLA_V,), F32)]), MLA_HEADS * (MLA_NOPE + MLA_V))
    tabs = _rope64_tables()

    def down(u, tok):
        kr_gain = _rope_slot_gain(g["kr"], tok.latent).reshape(1, LANES)
        return _mla_down(u, w["down"], (g["q_a"], g["kv_a"], kr_gain), tok)

    def keys_values(ckv, tok):
        return _head_proj("mla_kv", ckv, w["ukv"], (), 0, MLA_HEADS * (MLA_NOPE + MLA_V), ("N", "P"),
                          kv_gain, [], 0, tok, tn=1024)

    def queries(name, cq, tok):
        gain = jnp.concatenate([g["qn"], _rope_slot_gain(g["qr"], tok.latent)]) * scale
        return _head_proj(name, cq, w["uq"], (), 0, MLA_HEADS * MLA_Q_SLOT,
                          ("N", "NR" if tok.latent else "N"), _head_gain(gain, MLA_HEADS * MLA_Q_SLOT),
                          tabs if tok.latent else [], _ROPE64_SHIFT, tok, tn=1024)

    cq_l, ckv_l, kr_l = down(ul, lat)
    cq_c, ckv_c, kr_c = down(uc, ctx)
    kv_l = keys_values(ckv_l, lat)
    kv_c = keys_values(ckv_c, ctx)
    q_l = queries("mla_q", cq_l, lat)
    nope = lambda h: 2 * h
    val = lambda h: 2 * h + 1
    shared = lambda h: 0
    segs = [(CTX_LEN, [(kv_c, nope), (kr_c, shared)], (kv_c, val)),
            (SEQ, [(kv_l, nope), (kr_l, shared)], (kv_l, val))]
    o = _attention("mla", q_l, segs, q_rows=SEQ, n_kv=MLA_HEADS, group=1, dk=MLA_Q_SLOT, dv=MLA_V, tq=4096)
    hl = _out_proj("mla_out", o, w["o"], (), hl, mt, layer, lat)
    if not ctx_out:
        return hl, hc
    q_c = queries("mla_q_ctx", cq_c, ctx)
    oc = _attention("mla_ctx", q_c, segs[:1], q_rows=CTX_LEN, n_kv=MLA_HEADS, group=1, dk=MLA_Q_SLOT,
                    dv=MLA_V, tq=CTX_LEN)
    hc = _out_proj("mla_out_ctx", oc, w["o"], (), hc, mt, layer, ctx)
    return hl, hc


def _mla_weights(w_down, w_uq, w_ukv, w_o, j):
    down = jnp.concatenate([w_down[j], w_down[j][:, -MLA_ROPE:]], axis=1).astype(BF)
    uq = w_uq[j].reshape(MLA_Q_RANK, MLA_HEADS, MLA_QK)
    uq = jnp.concatenate([uq, uq[:, :, -MLA_ROPE:]], axis=2).astype(BF)
    return {"down": down, "uq": uq.reshape(MLA_Q_RANK, MLA_HEADS * MLA_Q_SLOT), "ukv": w_ukv[j], "o": w_o[j]}


def kernel(x, c, ctx, c_ctx, norm_g, w_mod, b_mod, ffn_w1, ffn_w2, ev_w_in, ev_w_out, na_q_g, na_k_g, na_rpb, gq_q_g, gq_k_g, mla_w_down, mla_q_a_g, mla_kv_a_g, mla_w_uq, mla_w_ukv, mla_qn_g, mla_qr_g, mla_kn_g, mla_kr_g, mla_w_o):
    lat = _Tokens(BATCH * SEQ, 1024, True)
    ctk = _Tokens(BATCH * CTX_LEN, 512, False)
    cvec = jnp.zeros((MOD_ROWS, D_MODEL), F32).at[:BATCH].set(c).at[CTX_MOD_ROW].set(c_ctx)
    mt = _modulation(cvec, w_mod, b_mod).reshape(DEPTH, MOD_ROWS, 1, N_MOD * D_MODEL)
    hl = x.reshape(BATCH * SEQ, D_MODEL)
    hc = ctx.reshape(BATCH * CTX_LEN, D_MODEL)
    for i in range(DEPTH):
        last = i == DEPTH - 1
        g = norm_g[i]
        hl, w2_bf = _ffn(hl, g[0], mt, i, 0, ffn_w1, ffn_w2, lat)
        hc, _ = _ffn(hc, g[0], mt, i, 0, ffn_w1, ffn_w2, ctk, w2_bf)
        ul = _normmod(hl, g[1], mt, i, 3, lat)
        uc = _normmod(hc, g[1], mt, i, 3, ctk)
        j = i // 2
        if i % 2 == 0:
            gains = (na_q_g[j], na_k_g[j], gq_q_g[j], gq_k_g[j])
            hl, hc = _even_mixer(ul, uc, hl, hc, mt, i, ev_w_in, ev_w_out, j, gains, na_rpb[j], lat, ctk,
                                 not last)
        else:
            w = _mla_weights(mla_w_down, mla_w_uq, mla_w_ukv, mla_w_o, j)
            gains = {"q_a": mla_q_a_g[j], "kv_a": mla_kv_a_g[j], "qn": mla_qn_g[j], "qr": mla_qr_g[j],
                     "kn": mla_kn_g[j], "kr": mla_kr_g[j]}
            hl, hc = _mla_mixer(ul, uc, hl, hc, mt, i, w, gains, lat, ctk, not last)
        hl, w2_bf = _ffn(hl, g[2], mt, i, 1, ffn_w1, ffn_w2, lat)
        if not last:
            hc, _ = _ffn(hc, g[2], mt, i, 1, ffn_w1, ffn_w2, ctk, w2_bf)
    return hl.reshape(BATCH, SEQ, D_MODEL)
```
